```python
import jax, jax.numpy as jnp
from jax import lax
import numpy as np

D_MODEL = 1024
BATCH = 16
SEQ = 2048
DEPTH = 4

GRID_W = 64
CTX_LEN = 256
N_MIXERS = 2

N_Q_HEADS = 16
N_KV_HEADS = 4
HEAD_DIM = 128
ATTN_WIDTH = N_Q_HEADS * HEAD_DIM
KV_WIDTH = N_KV_HEADS * HEAD_DIM
ATTN_PROJ = 2 * ATTN_WIDTH + 2 * KV_WIDTH
WINDOW = 128
BLOCK = 128
ROPE_BASE = 10000.0

CONV_WIDTH = 2 * D_MODEL
CONV_PROJ = 3 * CONV_WIDTH
CONV_K = 31

EPS = 1e-6

kernel_name = "hybrid_swa_sink_conformer_prefix_dit"


def rmsnorm(x, g):
    xf = x.astype(jnp.float32)
    y = xf * lax.rsqrt(jnp.mean(xf * xf, axis=-1, keepdims=True) + EPS)
    return (y * g.astype(jnp.float32)).astype(x.dtype)


def layernorm(x, g, b):
    xf = x.astype(jnp.float32)
    mu = jnp.mean(xf, axis=-1, keepdims=True)
    xc = xf - mu
    y = xc * lax.rsqrt(jnp.mean(xc * xc, axis=-1, keepdims=True) + EPS)
    return (y * g.astype(jnp.float32) + b.astype(jnp.float32)).astype(x.dtype)


def modulate(h, shift, scale):
    return h * (1 + scale) + shift


def axial_rope_tables(s):
    rows = s // GRID_W
    row = jnp.repeat(jnp.arange(rows), GRID_W).astype(jnp.float32)
    col = jnp.tile(jnp.arange(GRID_W), rows).astype(jnp.float32)
    n_axis = HEAD_DIM // 4
    inv = ROPE_BASE ** (-jnp.arange(n_axis, dtype=jnp.float32) / n_axis)
    ang = jnp.concatenate([row[:, None] * inv, col[:, None] * inv], axis=-1)
    return jnp.cos(ang), jnp.sin(ang)


def apply_rope(x, cos, sin):
    half = HEAD_DIM // 2
    xf = x.astype(jnp.float32)
    x1, x2 = xf[..., :half], xf[..., half:]
    cs, sn = cos[None, :, None, :], sin[None, :, None, :]
    return jnp.concatenate([x1 * cs - x2 * sn, x2 * cs + x1 * sn], axis=-1).astype(x.dtype)


def split_heads(t, n):
    return t.reshape(t.shape[0], t.shape[1], n, HEAD_DIM)


def sink_softmax(sink_l, *parts):
    first = parts[0]
    sink_col = jnp.broadcast_to(sink_l.astype(jnp.float32)[None, :, :, None, None], first.shape[:-1] + (1,))
    p = jax.nn.softmax(jnp.concatenate([sink_col, *parts], axis=-1), axis=-1)
    out, off = [], 1
    for part in parts:
        out.append(p[..., off:off + part.shape[-1]])
        off += part.shape[-1]
    return out


def context_attention(qc, kc, vc, sink):
    b, n = qc.shape[:2]
    g = N_Q_HEADS // N_KV_HEADS
    qg = qc.reshape(b, n, N_KV_HEADS, g, HEAD_DIM)
    sc = jnp.einsum('bqhgd,bkhd->bhgqk', qg, kc, preferred_element_type=jnp.float32) * (HEAD_DIM ** -0.5)
    (p,) = sink_softmax(sink.reshape(N_KV_HEADS, g), sc)
    o = jnp.einsum('bhgqk,bkhd->bqhgd', p.astype(vc.dtype), vc)
    return o.reshape(b, n, ATTN_WIDTH)


def windowed_attention(q, k, v, kc, vc, sink):
    b, s = q.shape[:2]
    nb = s // BLOCK
    g = N_Q_HEADS // N_KV_HEADS
    scale = HEAD_DIM ** -0.5
    span = BLOCK + 2 * WINDOW
    qb = jnp.moveaxis(q.reshape(b, nb, BLOCK, N_KV_HEADS, g, HEAD_DIM), 1, 0)
    pad = ((0, 0), (WINDOW, WINDOW), (0, 0), (0, 0))
    kp = jnp.pad(k, pad)
    vp = jnp.pad(v, pad)
    qi = jnp.arange(BLOCK)[:, None]
    kj = jnp.arange(span)[None, :]
    band = jnp.abs((kj - WINDOW) - qi) <= WINDOW
    sink_l = sink.reshape(N_KV_HEADS, g)

    def block_fn(args):
        qblk, blk = args
        start = blk * BLOCK
        kw = lax.dynamic_slice_in_dim(kp, start, span, axis=1)
        vw = lax.dynamic_slice_in_dim(vp, start, span, axis=1)
        key_pos = start - WINDOW + jnp.arange(span)
        mask = band & ((key_pos >= 0) & (key_pos < s))[None, :]
        s_win = jnp.einsum('bqhgd,bkhd->bhgqk', qblk, kw, preferred_element_type=jnp.float32) * scale
        s_win = jnp.where(mask, s_win, -jnp.inf)
        s_ctx = jnp.einsum('bqhgd,bchd->bhgqc', qblk, kc, preferred_element_type=jnp.float32) * scale
        p_ctx, p_win = sink_softmax(sink_l, s_ctx, s_win)
        o = (jnp.einsum('bhgqc,bchd->bqhgd', p_ctx.astype(vc.dtype), vc)
             + jnp.einsum('bhgqk,bkhd->bqhgd', p_win.astype(vw.dtype), vw))
        return o

    out = lax.map(block_fn, (qb, jnp.arange(nb)))
    return jnp.moveaxis(out, 0, 1).reshape(b, s, ATTN_WIDTH)


def attention_layer(h_lat, h_ctx, w_in, sink, w_out, cos, sin, ctx_out):
    q_end, k_end, v_end = ATTN_WIDTH, ATTN_WIDTH + KV_WIDTH, ATTN_WIDTH + 2 * KV_WIDTH
    proj = h_lat @ w_in
    q = apply_rope(split_heads(proj[..., :q_end], N_Q_HEADS), cos, sin)
    k = apply_rope(split_heads(proj[..., q_end:k_end], N_KV_HEADS), cos, sin)
    v = split_heads(proj[..., k_end:v_end], N_KV_HEADS)
    z = proj[..., v_end:]
    if ctx_out:
        proj_c = h_ctx @ w_in
        kc = split_heads(proj_c[..., q_end:k_end], N_KV_HEADS)
        vc = split_heads(proj_c[..., k_end:v_end], N_KV_HEADS)
        oc = context_attention(split_heads(proj_c[..., :q_end], N_Q_HEADS), kc, vc, sink)
        y_ctx = (oc * jax.nn.silu(proj_c[..., v_end:])) @ w_out
    else:
        kv_c = h_ctx @ w_in[:, q_end:v_end]
        kc = split_heads(kv_c[..., :KV_WIDTH], N_KV_HEADS)
        vc = split_heads(kv_c[..., KV_WIDTH:], N_KV_HEADS)
        y_ctx = None
    o = windowed_attention(q, k, v, kc, vc, sink)
    y = (o * jax.nn.silu(z)) @ w_out
    return y, y_ctx


def conformer_conv_branch(h, w_in, b_in, dw_w, dw_b, ln_g, ln_b, w_out):
    proj = h @ w_in + b_in
    a, gl, z = jnp.split(proj, 3, axis=-1)
    u = a * jax.nn.sigmoid(gl)
    u = lax.conv_general_dilated(u, dw_w[:, None, :], window_strides=(1,),
                                 padding=[(CONV_K // 2, CONV_K // 2)],
                                 dimension_numbers=('NWC', 'WIO', 'NWC'),
                                 feature_group_count=CONV_WIDTH) + dw_b
    u = jax.nn.silu(layernorm(u, ln_g, ln_b))
    return (u * jax.nn.silu(z)) @ w_out


def setup_inputs(seed: int = 0) -> dict:
    key = jax.random.key(seed)
    ks = jax.random.split(key, 20)
    n_attn = (DEPTH + N_MIXERS - 1) // N_MIXERS
    n_conv = DEPTH // N_MIXERS
    nrm = jax.random.normal
    f32 = jnp.float32
    return {
        "x": nrm(ks[0], (BATCH, SEQ, D_MODEL), f32),
        "c": nrm(ks[1], (BATCH, D_MODEL), f32),
        "ctx": nrm(ks[2], (BATCH, CTX_LEN, D_MODEL), f32),
        "c_ctx": nrm(ks[3], (D_MODEL,), f32),
        "ada_w": nrm(ks[4], (DEPTH, D_MODEL, 3 * D_MODEL), f32) * D_MODEL ** -0.5,
        "ada_b": nrm(ks[5], (DEPTH, 3 * D_MODEL), f32) * 0.01,
        "norm_g": 1.0 + 0.01 * nrm(ks[6], (DEPTH, D_MODEL), f32),
        "attn_w_in": nrm(ks[7], (n_attn, D_MODEL, ATTN_PROJ), f32) * D_MODEL ** -0.5,
        "attn_sink": nrm(ks[8], (n_attn, N_Q_HEADS), f32) * 0.5,
        "attn_w_out": nrm(ks[9], (n_attn, ATTN_WIDTH, D_MODEL), f32) * ATTN_WIDTH ** -0.5,
        "conv_w_in": nrm(ks[10], (n_conv, D_MODEL, CONV_PROJ), f32) * D_MODEL ** -0.5,
        "conv_b_in": nrm(ks[11], (n_conv, CONV_PROJ), f32) * 0.01,
        "conv_dw_w": nrm(ks[12], (n_conv, CONV_K, CONV_WIDTH), f32) * CONV_K ** -0.5,
        "conv_dw_b": nrm(ks[13], (n_conv, CONV_WIDTH), f32) * 0.01,
        "conv_ln_g": 1.0 + 0.01 * nrm(ks[14], (n_conv, CONV_WIDTH), f32),
        "conv_ln_b": nrm(ks[15], (n_conv, CONV_WIDTH), f32) * 0.01,
        "conv_w_out": nrm(ks[16], (n_conv, CONV_WIDTH, D_MODEL), f32) * CONV_WIDTH ** -0.5,
        "final_g": 1.0 + 0.01 * nrm(ks[17], (D_MODEL,), f32),
    }


def reference(x, c, ctx, c_ctx, ada_w, ada_b, norm_g, attn_w_in, attn_sink, attn_w_out,
              conv_w_in, conv_b_in, conv_dw_w, conv_dw_b, conv_ln_g, conv_ln_b, conv_w_out, final_g):
    s = x.shape[1]
    cos, sin = axial_rope_tables(s)
    silu_c = jax.nn.silu(c)
    silu_cc = jax.nn.silu(c_ctx)
    h_ctx_stream = ctx
    for i in range(DEPTH):
        kind = i % N_MIXERS
        j = i // N_MIXERS
        ctx_out = any(l % N_MIXERS == 0 for l in range(i + 1, DEPTH))
        need_ctx_in = (kind == 0) or ctx_out

        shift, scale, gate = jnp.split((silu_c @ ada_w[i] + ada_b[i])[:, None, :], 3, axis=-1)
        h_lat = modulate(rmsnorm(x, norm_g[i]), shift, scale)
        h_ctx = None
        if need_ctx_in:
            shift_c, scale_c, gate_c = jnp.split(silu_cc @ ada_w[i] + ada_b[i], 3, axis=-1)
            h_ctx = modulate(rmsnorm(h_ctx_stream, norm_g[i]), shift_c, scale_c)

        if kind == 0:
            y, y_ctx = attention_layer(h_lat, h_ctx, attn_w_in[j], attn_sink[j], attn_w_out[j],
                                       cos, sin, ctx_out)
        else:
            conv_args = (conv_w_in[j], conv_b_in[j], conv_dw_w[j], conv_dw_b[j],
                         conv_ln_g[j], conv_ln_b[j], conv_w_out[j])
            y = conformer_conv_branch(h_lat, *conv_args)
            y_ctx = conformer_conv_branch(h_ctx, *conv_args) if ctx_out else None

        x = x + gate * y
        if ctx_out:
            h_ctx_stream = h_ctx_stream + gate_c * y_ctx
    return rmsnorm(x, final_g)
```

```python
import functools

import jax
import jax.numpy as jnp
from jax import lax
from jax.experimental import pallas as pl
from jax.experimental.pallas import tpu as pltpu

D_MODEL = 1024
DEPTH = 4
GRID_W = 64
N_Q_HEADS = 16
N_KV_HEADS = 4
GROUP = N_Q_HEADS // N_KV_HEADS
HEAD_DIM = 128
ATTN_WIDTH = N_Q_HEADS * HEAD_DIM
KV_WIDTH = N_KV_HEADS * HEAD_DIM
ATTN_PROJ = 2 * ATTN_WIDTH + 2 * KV_WIDTH
WINDOW = 128
ROPE_BASE = 10000.0
CONV_WIDTH = 2 * D_MODEL
CONV_K = 31
CONV_HALF = CONV_K // 2
EPS = 1e-6

V7X_LANES = 128
V7X_SUBLANES = 8
V7X_VMEM_LIMIT_BYTES = 56 * 1024 * 1024

ROW_TILE = 512
COL_CHUNK = 512
Q_TILE = 128
CONV_HALO = 16
CONV_ROW_CHUNK = 64

BF16 = jnp.bfloat16
F32 = jnp.float32


def _params(*sem):
    return pltpu.CompilerParams(dimension_semantics=sem, vmem_limit_bytes=V7X_VMEM_LIMIT_BYTES)


def _silu(t):
    return t * jax.nn.sigmoid(t)


def _resident(shape):
    nd = len(shape)
    return pl.BlockSpec(shape, lambda *_: (0,) * nd, pipeline_mode=pl.Buffered(1))


def _ada_kernel(c_ref, w_ref, b_ref, o_ref):
    a = _silu(c_ref[...])
    o_ref[0] = jnp.dot(a.astype(BF16), w_ref[0].astype(BF16), preferred_element_type=F32) + b_ref[0]


def _ada_call(c_rows, ada_w, ada_b):
    n_rows = c_rows.shape[0]
    tn = D_MODEL
    return pl.pallas_call(
        _ada_kernel,
        grid=(DEPTH, 3 * D_MODEL // tn),
        in_specs=[
            pl.BlockSpec((n_rows, D_MODEL), lambda i, j: (0, 0)),
            pl.BlockSpec((1, D_MODEL, tn), lambda i, j: (i, 0, j)),
            pl.BlockSpec((1, 1, tn), lambda i, j: (i, 0, j)),
        ],
        out_specs=pl.BlockSpec((1, n_rows, tn), lambda i, j: (i, 0, j)),
        out_shape=jax.ShapeDtypeStruct((DEPTH, n_rows, 3 * D_MODEL), F32),
        compiler_params=_params("arbitrary", "arbitrary"),
        name="ada_mod",
    )(c_rows, ada_w, ada_b.reshape(DEPTH, 1, 3 * D_MODEL))


def _modulated_norm(x, g_row, mod):
    ms = jnp.mean(x * x, axis=-1, keepdims=True)
    y = x * lax.rsqrt(ms + EPS) * g_row
    return y * (1.0 + mod[1:2, :]) + mod[0:1, :]


def _rope(t, cos, sin_signed):
    heads = []
    for h in range(t.shape[1] // HEAD_DIM):
        th = t[:, h * HEAD_DIM:(h + 1) * HEAD_DIM]
        heads.append(th * cos + pltpu.roll(th, HEAD_DIM // 2, 1) * sin_signed)
    return jnp.concatenate(heads, axis=1)


def _attn_inproj_kernel(x_ref, mod_ref, g_ref, w_ref, cos_ref, sin_ref, o_ref, *, kinds, rope):
    hb = _modulated_norm(x_ref[0], g_ref[...], mod_ref[0]).astype(BF16)
    for j, kind in enumerate(kinds):
        cols = slice(j * COL_CHUNK, (j + 1) * COL_CHUNK)
        t = jnp.dot(hb, w_ref[:, cols], preferred_element_type=F32)
        if rope and kind in ("q", "k"):
            t = _rope(t, cos_ref[...], sin_ref[...])
        if kind == "q":
            t = t * (HEAD_DIM ** -0.5)
        if kind == "z":
            t = _silu(t)
        o_ref[0, :, cols] = t.astype(BF16)


def _attn_inproj_call(x, mod, g_row, w, cos, sin_signed, *, kinds, rope, row_tile):
    b, n, _ = x.shape
    width = len(kinds) * COL_CHUNK
    assert w.shape == (D_MODEL, width) and n % row_tile == 0
    mod_map = (lambda bi, ti: (bi, 0, 0)) if mod.shape[0] == b else (lambda bi, ti: (0, 0, 0))
    return pl.pallas_call(
        functools.partial(_attn_inproj_kernel, kinds=kinds, rope=rope),
        grid=(b, n // row_tile),
        in_specs=[
            pl.BlockSpec((1, row_tile, D_MODEL), lambda bi, ti: (bi, ti, 0)),
            pl.BlockSpec((1, 3, D_MODEL), mod_map),
            _resident((1, D_MODEL)),
            _resident((D_MODEL, width)),
            pl.BlockSpec((row_tile, HEAD_DIM), lambda bi, ti: (ti, 0)),
            pl.BlockSpec((row_tile, HEAD_DIM), lambda bi, ti: (ti, 0)),
        ],
        out_specs=pl.BlockSpec((1, row_tile, width), lambda bi, ti: (bi, ti, 0)),
        out_shape=jax.ShapeDtypeStruct((b, n, width), BF16),
        compiler_params=_params("parallel", "parallel"),
        name="attn_inproj",
    )(x, mod, g_row, w, cos, sin_signed)


def _dot_nt(a, b):
    return lax.dot_general(a, b, (((1,), (1,)), ((), ())), preferred_element_type=F32)


def _attn_kernel(sink_ref, q_ref, z_ref, kc_ref, vc_ref, *rest, window, q_tile):
    if window:
        kp_ref, km_ref, kn_ref, vp_ref, vm_ref, vn_ref, o_ref = rest
    else:
        (o_ref,) = rest
    hk = pl.program_id(1)
    q = q_ref[0]
    qs = jnp.concatenate([q[:, g * HEAD_DIM:(g + 1) * HEAD_DIM] for g in range(GROUP)], axis=0)
    rows = GROUP * q_tile

    scores = [_dot_nt(qs, kc_ref[0])]
    values = [vc_ref[0]]
    if window:
        ti = pl.program_id(2)
        n_tiles = pl.num_programs(2)
        r = lax.broadcasted_iota(jnp.int32, (rows, WINDOW), 0) & (q_tile - 1)
        j = lax.broadcasted_iota(jnp.int32, (rows, WINDOW), 1)
        neg = jnp.float32(-jnp.inf)
        s_prev = jnp.where((j >= r) & (ti > 0), _dot_nt(qs, kp_ref[0]), neg)
        s_next = jnp.where((j <= r) & (ti < n_tiles - 1), _dot_nt(qs, kn_ref[0]), neg)
        scores += [s_prev, _dot_nt(qs, km_ref[0]), s_next]
        values += [vp_ref[0], vm_ref[0], vn_ref[0]]

    sink_col = jnp.concatenate(
        [jnp.full((q_tile, 1), sink_ref[hk * GROUP + g], F32) for g in range(GROUP)], axis=0)
    m = sink_col
    for s in scores:
        m = jnp.maximum(m, jnp.max(s, axis=-1, keepdims=True))
    denom = jnp.exp(sink_col - m)
    acc = jnp.zeros((rows, HEAD_DIM), F32)
    for s, v in zip(scores, values):
        p = jnp.exp(s - m)
        denom = denom + jnp.sum(p, axis=-1, keepdims=True)
        acc = acc + jnp.dot(p.astype(BF16), v, preferred_element_type=F32)
    o = acc / denom
    for g in range(GROUP):
        cols = slice(g * HEAD_DIM, (g + 1) * HEAD_DIM)
        gate = z_ref[0, :, cols].astype(F32)
        o_ref[0, :, cols] = (o[g * q_tile:(g + 1) * q_tile, :] * gate).astype(BF16)


def _attn_call(sink, proj, proj_ctx, *, window, ctx_k_col, ctx_v_col):
    b, n, _ = proj.shape
    n_ctx = proj_ctx.shape[1]
    q_tile = Q_TILE if window else n
    n_tiles = n // q_tile
    hd_blocks = ATTN_WIDTH // HEAD_DIM
    k_blk, v_blk = hd_blocks, hd_blocks + N_KV_HEADS
    z_blk = (ATTN_WIDTH + 2 * KV_WIDTH) // (GROUP * HEAD_DIM)
    in_specs = [
        pl.BlockSpec(memory_space=pltpu.SMEM),
        pl.BlockSpec((1, q_tile, GROUP * HEAD_DIM), lambda bi, hk, ti: (bi, ti, hk)),
        pl.BlockSpec((1, q_tile, GROUP * HEAD_DIM), lambda bi, hk, ti: (bi, ti, z_blk + hk)),
        pl.BlockSpec((1, n_ctx, HEAD_DIM), lambda bi, hk, ti: (bi, 0, ctx_k_col // HEAD_DIM + hk)),
        pl.BlockSpec((1, n_ctx, HEAD_DIM), lambda bi, hk, ti: (bi, 0, ctx_v_col // HEAD_DIM + hk)),
    ]
    args = [sink, proj, proj, proj_ctx, proj_ctx]
    if window:
        assert q_tile == WINDOW
        for blk in (k_blk, v_blk):
            in_specs += [
                pl.BlockSpec((1, WINDOW, HEAD_DIM),
                             lambda bi, hk, ti, blk=blk: (bi, jnp.maximum(ti - 1, 0), blk + hk)),
                pl.BlockSpec((1, q_tile, HEAD_DIM), lambda bi, hk, ti, blk=blk: (bi, ti, blk + hk)),
                pl.BlockSpec((1, WINDOW, HEAD_DIM),
                             lambda bi, hk, ti, blk=blk: (bi, jnp.minimum(ti + 1, n_tiles - 1), blk + hk)),
            ]
            args += [proj, proj, proj]
    return pl.pallas_call(
        functools.partial(_attn_kernel, window=window, q_tile=q_tile),
        grid=(b, N_KV_HEADS, n_tiles),
        in_specs=in_specs,
        out_specs=pl.BlockSpec((1, q_tile, GROUP * HEAD_DIM), lambda bi, hk, ti: (bi, ti, hk)),
        out_shape=jax.ShapeDtypeStruct((b, n, ATTN_WIDTH), BF16),
        compiler_params=_params("parallel", "parallel", "arbitrary"),
        name="attn_window" if window else "attn_ctx",
    )(*args)


def _residual_out(x, y, mod, final_g):
    out = x + mod[2:3, :] * y
    if final_g is not None:
        ms = jnp.mean(out * out, axis=-1, keepdims=True)
        out = out * lax.rsqrt(ms + EPS) * final_g
    return out


def _outproj_kernel(a_ref, w_ref, x_ref, mod_ref, o_ref):
    y = jnp.dot(a_ref[0], w_ref[...], preferred_element_type=F32)
    o_ref[0] = _residual_out(x_ref[0], y, mod_ref[0], None)


def _outproj_call(a, w, x, mod, *, row_tile):
    b, n, width = a.shape
    mod_map = (lambda bi, ti: (bi, 0, 0)) if mod.shape[0] == b else (lambda bi, ti: (0, 0, 0))
    return pl.pallas_call(
        _outproj_kernel,
        grid=(b, n // row_tile),
        in_specs=[
            pl.BlockSpec((1, row_tile, width), lambda bi, ti: (bi, ti, 0)),
            _resident((width, D_MODEL)),
            pl.BlockSpec((1, row_tile, D_MODEL), lambda bi, ti: (bi, ti, 0)),
            pl.BlockSpec((1, 3, D_MODEL), mod_map),
        ],
        out_specs=pl.BlockSpec((1, row_tile, D_MODEL), lambda bi, ti: (bi, ti, 0)),
        out_shape=jax.ShapeDtypeStruct((b, n, D_MODEL), F32),
        compiler_params=_params("parallel", "parallel"),
        name="attn_outproj",
    )(a, w, x, mod)


def _conv_inproj_kernel(x_ref, mod_ref, g_ref, w_ref, b_ref, u_ref, z_ref):
    hb = _modulated_norm(x_ref[0], g_ref[...], mod_ref[0]).astype(BF16)
    for j in range(CONV_WIDTH // COL_CHUNK):
        ca = slice(j * COL_CHUNK, (j + 1) * COL_CHUNK)
        cg = slice(CONV_WIDTH + j * COL_CHUNK, CONV_WIDTH + (j + 1) * COL_CHUNK)
        cz = slice(2 * CONV_WIDTH + j * COL_CHUNK, 2 * CONV_WIDTH + (j + 1) * COL_CHUNK)
        a = jnp.dot(hb, w_ref[:, ca], preferred_element_type=F32) + b_ref[:, ca]
        gl = jnp.dot(hb, w_ref[:, cg], preferred_element_type=F32) + b_ref[:, cg]
        u_ref[0, :, ca] = (a * jax.nn.sigmoid(gl)).astype(BF16)
        z = jnp.dot(hb, w_ref[:, cz], preferred_element_type=F32) + b_ref[:, cz]
        z_ref[0, :, ca] = _silu(z).astype(BF16)


def _conv_inproj_call(x, mod, g_row, w, bias, *, row_tile):
    b, n, _ = x.shape
    mod_map = (lambda bi, ti: (bi, 0, 0)) if mod.shape[0] == b else (lambda bi, ti: (0, 0, 0))
    out_spec = pl.BlockSpec((1, row_tile, CONV_WIDTH), lambda bi, ti: (bi, ti, 0))
    out_sds = jax.ShapeDtypeStruct((b, n, CONV_WIDTH), BF16)
    return pl.pallas_call(
        _conv_inproj_kernel,
        grid=(b, n // row_tile),
        in_specs=[
            pl.BlockSpec((1, row_tile, D_MODEL), lambda bi, ti: (bi, ti, 0)),
            pl.BlockSpec((1, 3, D_MODEL), mod_map),
            _resident((1, D_MODEL)),
            _resident((D_MODEL, 3 * CONV_WIDTH)),
            _resident((1, 3 * CONV_WIDTH)),
        ],
        out_specs=[out_spec, out_spec],
        out_shape=[out_sds, out_sds],
        compiler_params=_params("parallel", "parallel"),
        name="conv_inproj",
    )(x, mod, g_row, w, bias)


def _conv_mix_kernel(up_ref, um_ref, un_ref, z_ref, dww_ref, dwb_ref, lng_ref, lnb_ref, w_ref,
                     x_ref, mod_ref, fg_ref, o_ref, win_ref, act_ref, *, row_tile, final):
    ti = pl.program_id(1)
    n_tiles = pl.num_programs(1)
    prev = jnp.where(ti > 0, up_ref[0].astype(F32), 0.0)
    nxt = jnp.where(ti < n_tiles - 1, un_ref[0].astype(F32), 0.0)
    win_ref[0:CONV_HALO, :] = prev
    win_ref[CONV_HALO:CONV_HALO + row_tile, :] = um_ref[0].astype(F32)
    win_ref[CONV_HALO + row_tile:, :] = nxt

    n_lane_chunks = CONV_WIDTH // V7X_LANES
    first = CONV_HALO - CONV_HALF

    def chunk(idx, carry):
        r0 = pl.multiple_of((idx // n_lane_chunks) * CONV_ROW_CHUNK, CONV_ROW_CHUNK)
        c0 = pl.multiple_of((idx % n_lane_chunks) * V7X_LANES, V7X_LANES)
        lanes = pl.ds(c0, V7X_LANES)
        span = CONV_ROW_CHUNK + 2 * CONV_HALO
        w = win_ref[pl.ds(r0, span), lanes]
        acc = jnp.zeros((CONV_ROW_CHUNK, V7X_LANES), F32)
        for r in range(V7X_SUBLANES):
            wr = w if r == 0 else pltpu.roll(w, span - r, 0)
            for k in range(CONV_K):
                shift = first + k
                if shift % V7X_SUBLANES == r:
                    a0 = shift - r
                    acc = acc + wr[a0:a0 + CONV_ROW_CHUNK, :] * dww_ref[k:k + 1, lanes]
        act_ref[pl.ds(r0, CONV_ROW_CHUNK), lanes] = acc + dwb_ref[:, lanes]
        return carry

    lax.fori_loop(0, (row_tile // CONV_ROW_CHUNK) * n_lane_chunks, chunk, 0)

    u = act_ref[...]
    mu = jnp.mean(u, axis=-1, keepdims=True)
    uc = u - mu
    var = jnp.mean(uc * uc, axis=-1, keepdims=True)
    t = uc * lax.rsqrt(var + EPS) * lng_ref[...] + lnb_ref[...]
    a = (_silu(t) * z_ref[0].astype(F32)).astype(BF16)
    y = jnp.dot(a, w_ref[...], preferred_element_type=F32)
    o_ref[0] = _residual_out(x_ref[0], y, mod_ref[0], fg_ref[...] if final else None)


def _conv_mix_call(u, z, dw_w, dw_b, ln_g, ln_b, w_out, x, mod, final_g, *, row_tile, final):
    b, n, _ = u.shape
    assert n % row_tile == 0 and row_tile % CONV_HALO == 0 and row_tile % CONV_ROW_CHUNK == 0
    halo_per_tile = row_tile // CONV_HALO
    n_halo_blocks = n // CONV_HALO
    mod_map = (lambda bi, ti: (bi, 0, 0)) if mod.shape[0] == b else (lambda bi, ti: (0, 0, 0))
    return pl.pallas_call(
        functools.partial(_conv_mix_kernel, row_tile=row_tile, final=final),
        grid=(b, n // row_tile),
        in_specs=[
            pl.BlockSpec((1, CONV_HALO, CONV_WIDTH),
                         lambda bi, ti: (bi, jnp.maximum(ti * halo_per_tile - 1, 0), 0)),
            pl.BlockSpec((1, row_tile, CONV_WIDTH), lambda bi, ti: (bi, ti, 0)),
            pl.BlockSpec((1, CONV_HALO, CONV_WIDTH),
                         lambda bi, ti: (bi, jnp.minimum((ti + 1) * halo_per_tile, n_halo_blocks - 1), 0)),
            pl.BlockSpec((1, row_tile, CONV_WIDTH), lambda bi, ti: (bi, ti, 0)),
            _resident((CONV_K, CONV_WIDTH)),
            _resident((1, CONV_WIDTH)),
            _resident((1, CONV_WIDTH)),
            _resident((1, CONV_WIDTH)),
            _resident((CONV_WIDTH, D_MODEL)),
            pl.BlockSpec((1, row_tile, D_MODEL), lambda bi, ti: (bi, ti, 0)),
            pl.BlockSpec((1, 3, D_MODEL), mod_map),
            _resident((1, D_MODEL)),
        ],
        out_specs=pl.BlockSpec((1, row_tile, D_MODEL), lambda bi, ti: (bi, ti, 0)),
        out_shape=jax.ShapeDtypeStruct((b, n, D_MODEL), F32),
        scratch_shapes=[
            pltpu.VMEM((row_tile + 2 * CONV_HALO, CONV_WIDTH), F32),
            pltpu.VMEM((row_tile, CONV_WIDTH), F32),
        ],
        compiler_params=_params("parallel", "arbitrary"),
        name="conv_mix",
    )(u, u, u, z, dw_w, dw_b, ln_g, ln_b, w_out, x, mod, final_g)


def _rope_tables(s):
    rows = s // GRID_W
    row = jnp.repeat(jnp.arange(rows), GRID_W).astype(F32)
    col = jnp.tile(jnp.arange(GRID_W), rows).astype(F32)
    n_axis = HEAD_DIM // 4
    inv = ROPE_BASE ** (-jnp.arange(n_axis, dtype=F32) / n_axis)
    ang = jnp.concatenate([row[:, None] * inv, col[:, None] * inv], axis=-1)
    cos, sin = jnp.cos(ang), jnp.sin(ang)
    return jnp.concatenate([cos, cos], axis=-1), jnp.concatenate([-sin, sin], axis=-1)


def kernel(x, c, ctx, c_ctx, ada_w, ada_b, norm_g, attn_w_in, attn_sink, attn_w_out,
           conv_w_in, conv_b_in, conv_dw_w, conv_dw_b, conv_ln_g, conv_ln_b, conv_w_out, final_g):
    b, s, _ = x.shape
    n_ctx = ctx.shape[1]
    cos, sin_signed = _rope_tables(s)
    ones_ctx = jnp.ones((n_ctx, HEAD_DIM), F32)
    zeros_ctx = jnp.zeros((n_ctx, HEAD_DIM), F32)

    n_rows = -(-(b + 1) // V7X_SUBLANES) * V7X_SUBLANES
    c_rows = jnp.concatenate([c, c_ctx[None, :], jnp.zeros((n_rows - b - 1, D_MODEL), F32)], axis=0)
    mods = _ada_call(c_rows, ada_w, ada_b)
    mods = mods.reshape(DEPTH, n_rows, 3, D_MODEL)

    final_row = final_g.reshape(1, D_MODEL)
    attn_kinds = ("q",) * 4 + ("k", "v") + ("z",) * 4
    ctx_stream = ctx
    for i in range(DEPTH):
        kind, j = i % 2, i // 2
        ctx_out = any(l % 2 == 0 for l in range(i + 1, DEPTH))
        mod_lat = mods[i, :b]
        mod_ctx = mods[i, b:b + 1]
        g_row = norm_g[i].reshape(1, D_MODEL)
        if kind == 0:
            w_in = attn_w_in[j].astype(BF16)
            w_out = attn_w_out[j].astype(BF16)
            proj = _attn_inproj_call(x, mod_lat, g_row, w_in, cos, sin_signed,
                                     kinds=attn_kinds, rope=True, row_tile=ROW_TILE)
            if ctx_out:
                proj_c = _attn_inproj_call(ctx_stream, mod_ctx, g_row, w_in, ones_ctx, zeros_ctx,
                                           kinds=attn_kinds, rope=False, row_tile=n_ctx)
                k_col, v_col = ATTN_WIDTH, ATTN_WIDTH + KV_WIDTH
                o_c = _attn_call(attn_sink[j], proj_c, proj_c, window=False, ctx_k_col=k_col, ctx_v_col=v_col)
                ctx_next = _outproj_call(o_c, w_out, ctx_stream, mod_ctx, row_tile=n_ctx)
            else:
                proj_c = _attn_inproj_call(ctx_stream, mod_ctx, g_row,
                                           w_in[:, ATTN_WIDTH:ATTN_WIDTH + 2 * KV_WIDTH],
                                           ones_ctx, zeros_ctx, kinds=("k", "v"), rope=False, row_tile=n_ctx)
                k_col, v_col = 0, KV_WIDTH
                ctx_next = None
            o = _attn_call(attn_sink[j], proj, proj_c, window=True, ctx_k_col=k_col, ctx_v_col=v_col)
            x = _outproj_call(o, w_out, x, mod_lat, row_tile=ROW_TILE)
        else:
            w_in = conv_w_in[j].astype(BF16)
            w_out = conv_w_out[j].astype(BF16)
            bias = conv_b_in[j].reshape(1, 3 * CONV_WIDTH)
            conv_args = (conv_dw_w[j], conv_dw_b[j].reshape(1, CONV_WIDTH),
                         conv_ln_g[j].reshape(1, CONV_WIDTH), conv_ln_b[j].reshape(1, CONV_WIDTH), w_out)
            final = i == DEPTH - 1
            if ctx_out:
                u_c, z_c = _conv_inproj_call(ctx_stream, mod_ctx, g_row, w_in, bias, row_tile=n_ctx)
                ctx_next = _conv_mix_call(u_c, z_c, *conv_args, ctx_stream, mod_ctx, final_row,
                                          row_tile=n_ctx, final=False)
            else:
                ctx_next = None
            u, z = _conv_inproj_call(x, mod_lat, g_row, w_in, bias, row_tile=ROW_TILE)
            x = _conv_mix_call(u, z, *conv_args, x, mod_lat, final_row, row_tile=ROW_TILE, final=final)
        if ctx_next is not None:
            ctx_stream = ctx_next
    return x
```

```python
import functools
import math

import jax
import jax.numpy as jnp
from jax import lax
from jax.experimental import pallas as pl
from jax.experimental.pallas import tpu as pltpu

D_MODEL = 1024
DEPTH = 4
GRID_W = 64
N_Q_HEADS = 16
N_KV_HEADS = 4
GROUP = N_Q_HEADS // N_KV_HEADS
HEAD_DIM = 128
ATTN_WIDTH = N_Q_HEADS * HEAD_DIM
KV_WIDTH = N_KV_HEADS * HEAD_DIM
ATTN_PROJ = 2 * ATTN_WIDTH + 2 * KV_WIDTH
WINDOW = 128
ROPE_BASE = 10000.0
CONV_WIDTH = 2 * D_MODEL
CONV_K = 31
CONV_HALF = CONV_K // 2
EPS = 1e-6
LOG2E = math.log2(math.e)

V7X_LANES = 128
V7X_SUBLANES = 8
V7X_VMEM_LIMIT_BYTES = 56 * 1024 * 1024

ROW_TILE = 512
COL_CHUNK = 512
Q_TILE = 2 * WINDOW
CONV_HALO = 16
CONV_ROW_CHUNK = 64

Q_COL, Z_COL, K_COL, V_COL = 0, ATTN_WIDTH, 2 * ATTN_WIDTH, 2 * ATTN_WIDTH + KV_WIDTH

BF16 = jnp.bfloat16
F32 = jnp.float32


def _params(*sem):
    return pltpu.CompilerParams(dimension_semantics=sem, vmem_limit_bytes=V7X_VMEM_LIMIT_BYTES)


def _silu(t):
    return t * jax.nn.sigmoid(t)


def _resident(shape):
    nd = len(shape)
    return pl.BlockSpec(shape, lambda *_: (0,) * nd, pipeline_mode=pl.Buffered(1))


def _ada_kernel(c_ref, w_ref, b_ref, o_ref):
    a = _silu(c_ref[...])
    o_ref[0] = jnp.dot(a.astype(BF16), w_ref[0].astype(BF16), preferred_element_type=F32) + b_ref[0]


def _ada_call(c_rows, ada_w, ada_b):
    n_rows = c_rows.shape[0]
    tn = D_MODEL
    return pl.pallas_call(
        _ada_kernel,
        grid=(DEPTH, 3 * D_MODEL // tn),
        in_specs=[
            pl.BlockSpec((n_rows, D_MODEL), lambda i, j: (0, 0)),
            pl.BlockSpec((1, D_MODEL, tn), lambda i, j: (i, 0, j)),
            pl.BlockSpec((1, 1, tn), lambda i, j: (i, 0, j)),
        ],
        out_specs=pl.BlockSpec((1, n_rows, tn), lambda i, j: (i, 0, j)),
        out_shape=jax.ShapeDtypeStruct((DEPTH, n_rows, 3 * D_MODEL), F32),
        compiler_params=_params("arbitrary", "arbitrary"),
        name="ada_mod",
    )(c_rows, ada_w, ada_b.reshape(DEPTH, 1, 3 * D_MODEL))


def _modulated_norm(x, g_row, mod):
    ms = jnp.mean(x * x, axis=-1, keepdims=True)
    y = x * lax.rsqrt(ms + EPS) * g_row
    return y * (1.0 + mod[1:2, :]) + mod[0:1, :]


def _rope(t, cos, sin_signed):
    heads = []
    for h in range(t.shape[1] // HEAD_DIM):
        th = t[:, h * HEAD_DIM:(h + 1) * HEAD_DIM]
        heads.append(th * cos + pltpu.roll(th, HEAD_DIM // 2, 1) * sin_signed)
    return jnp.concatenate(heads, axis=1)


def _attn_inproj_kernel(x_ref, mod_ref, g_ref, w_ref, cos_ref, sin_ref, o_ref, *, kinds, rope):
    hb = _modulated_norm(x_ref[0], g_ref[...], mod_ref[0]).astype(BF16)
    for j, kind in enumerate(kinds):
        cols = slice(j * COL_CHUNK, (j + 1) * COL_CHUNK)
        t = jnp.dot(hb, w_ref[:, cols], preferred_element_type=F32)
        if rope and kind in ("q", "k"):
            t = _rope(t, cos_ref[...], sin_ref[...])
        if kind == "q":
            t = t * (HEAD_DIM ** -0.5 * LOG2E)
        if kind == "z":
            t = _silu(t)
        o_ref[0, :, cols] = t.astype(BF16)


def _attn_inproj_call(x, mod, g_row, w, cos, sin_signed, *, kinds, rope, row_tile):
    b, n, _ = x.shape
    width = len(kinds) * COL_CHUNK
    assert w.shape == (D_MODEL, width) and n % row_tile == 0
    mod_map = (lambda bi, ti: (bi, 0, 0)) if mod.shape[0] == b else (lambda bi, ti: (0, 0, 0))
    return pl.pallas_call(
        functools.partial(_attn_inproj_kernel, kinds=kinds, rope=rope),
        grid=(b, n // row_tile),
        in_specs=[
            pl.BlockSpec((1, row_tile, D_MODEL), lambda bi, ti: (bi, ti, 0)),
            pl.BlockSpec((1, 3, D_MODEL), mod_map),
            _resident((1, D_MODEL)),
            _resident((D_MODEL, width)),
            pl.BlockSpec((row_tile, HEAD_DIM), lambda bi, ti: (ti, 0)),
            pl.BlockSpec((row_tile, HEAD_DIM), lambda bi, ti: (ti, 0)),
        ],
        out_specs=pl.BlockSpec((1, row_tile, width), lambda bi, ti: (bi, ti, 0)),
        out_shape=jax.ShapeDtypeStruct((b, n, width), BF16),
        compiler_params=_params("parallel", "parallel"),
        name="attn_inproj",
    )(x, mod, g_row, w, cos, sin_signed)


def _dot_nt(a, b):
    return lax.dot_general(a, b, (((1,), (1,)), ((), ())), preferred_element_type=F32)


def _attn_kernel(sink_ref, q_ref, z_ref, kc_ref, vc_ref, *rest, window, q_tile, n_ctx):
    if window:
        kp_ref, km_ref, kn_ref, vp_ref, vm_ref, vn_ref, bias_ref, o_ref, k_all, v_all = rest
        for dst, parts in ((k_all, (kc_ref, kp_ref, km_ref, kn_ref)), (v_all, (vc_ref, vp_ref, vm_ref, vn_ref))):
            row = 0
            for part in parts:
                dst[row:row + part.shape[1], :] = part[0]
                row += part.shape[1]
        bias = bias_ref[0]
    else:
        (o_ref,) = rest

    for hk in range(N_KV_HEADS):
        kv_cols = slice(hk * HEAD_DIM, (hk + 1) * HEAD_DIM)
        qs = jnp.concatenate(
            [q_ref[0, :, (hk * GROUP + g) * HEAD_DIM:(hk * GROUP + g + 1) * HEAD_DIM] for g in range(GROUP)],
            axis=0)
        k = k_all[:, kv_cols] if window else kc_ref[0, :, kv_cols]
        v = v_all[:, kv_cols] if window else vc_ref[0, :, kv_cols]
        s = _dot_nt(qs, k)
        probs, denoms = [], []
        for g in range(GROUP):
            sg = s[g * q_tile:(g + 1) * q_tile, :]
            sink = jnp.full((q_tile, 1), sink_ref[hk * GROUP + g] * LOG2E, F32)
            if window:
                s_ctx, s_win = sg[:, :n_ctx], sg[:, n_ctx:] + bias
                m = jnp.maximum(jnp.max(s_ctx, axis=-1, keepdims=True), jnp.max(s_win, axis=-1, keepdims=True))
                m = jnp.maximum(m, sink)
                p_ctx, p_win = jnp.exp2(s_ctx - m), jnp.exp2(s_win - m)
                denom = (jnp.exp2(sink - m) + jnp.sum(p_ctx, axis=-1, keepdims=True)
                         + jnp.sum(p_win, axis=-1, keepdims=True))
                p = jnp.concatenate([p_ctx.astype(BF16), p_win.astype(BF16)], axis=1)
            else:
                m = jnp.maximum(jnp.max(sg, axis=-1, keepdims=True), sink)
                p = jnp.exp2(sg - m)
                denom = jnp.exp2(sink - m) + jnp.sum(p, axis=-1, keepdims=True)
                p = p.astype(BF16)
            probs.append(p)
            denoms.append(denom)
        o = jnp.dot(jnp.concatenate(probs, axis=0), v, preferred_element_type=F32)
        for g in range(GROUP):
            cols = slice((hk * GROUP + g) * HEAD_DIM, (hk * GROUP + g + 1) * HEAD_DIM)
            gate = z_ref[0, :, cols].astype(F32)
            o_ref[0, :, cols] = (o[g * q_tile:(g + 1) * q_tile, :] / denoms[g] * gate).astype(BF16)


def _band_bias(q_tile):
    n_win = q_tile + 2 * WINDOW
    r = jnp.arange(q_tile)[:, None]
    j = jnp.arange(n_win)[None, :]
    band = jnp.abs(j - WINDOW - r) <= WINDOW
    first = band & (j >= WINDOW)
    last = band & (j < WINDOW + q_tile)
    neg = jnp.float32(-jnp.inf)
    return jnp.stack([jnp.where(msk, 0.0, neg).astype(F32) for msk in (first, band, last)])


def _attn_call(sink, proj, proj_ctx, *, window, ctx_k_col, ctx_v_col):
    b, n, _ = proj.shape
    n_ctx = proj_ctx.shape[1]
    q_tile = Q_TILE if window else n
    n_tiles = n // q_tile
    assert n % q_tile == 0 and (not window or n_tiles >= 2)
    half_per_tile = q_tile // WINDOW
    n_half = n // WINDOW
    in_specs = [
        pl.BlockSpec(memory_space=pltpu.SMEM),
        pl.BlockSpec((1, q_tile, ATTN_WIDTH), lambda bi, ti: (bi, ti, Q_COL // ATTN_WIDTH)),
        pl.BlockSpec((1, q_tile, ATTN_WIDTH), lambda bi, ti: (bi, ti, Z_COL // ATTN_WIDTH)),
        pl.BlockSpec((1, n_ctx, KV_WIDTH), lambda bi, ti: (bi, 0, ctx_k_col // KV_WIDTH)),
        pl.BlockSpec((1, n_ctx, KV_WIDTH), lambda bi, ti: (bi, 0, ctx_v_col // KV_WIDTH)),
    ]
    args = [sink, proj, proj, proj_ctx, proj_ctx]
    scratch = []
    if window:
        for col in (K_COL // KV_WIDTH, V_COL // KV_WIDTH):
            in_specs += [
                pl.BlockSpec((1, WINDOW, KV_WIDTH),
                             lambda bi, ti, col=col: (bi, jnp.maximum(ti * half_per_tile - 1, 0), col)),
                pl.BlockSpec((1, q_tile, KV_WIDTH), lambda bi, ti, col=col: (bi, ti, col)),
                pl.BlockSpec((1, WINDOW, KV_WIDTH),
                             lambda bi, ti, col=col: (bi, jnp.minimum((ti + 1) * half_per_tile, n_half - 1), col)),
            ]
            args += [proj, proj, proj]
        in_specs.append(pl.BlockSpec(
            (1, q_tile, q_tile + 2 * WINDOW),
            lambda bi, ti: (jnp.where(ti == 0, 0, jnp.where(ti == n_tiles - 1, 2, 1)), 0, 0)))
        args.append(_band_bias(q_tile))
        n_keys = n_ctx + q_tile + 2 * WINDOW
        scratch = [pltpu.VMEM((n_keys, KV_WIDTH), BF16), pltpu.VMEM((n_keys, KV_WIDTH), BF16)]
    return pl.pallas_call(
        functools.partial(_attn_kernel, window=window, q_tile=q_tile, n_ctx=n_ctx),
        grid=(b, n_tiles),
        in_specs=in_specs,
        out_specs=pl.BlockSpec((1, q_tile, ATTN_WIDTH), lambda bi, ti: (bi, ti, 0)),
        out_shape=jax.ShapeDtypeStruct((b, n, ATTN_WIDTH), BF16),
        scratch_shapes=scratch,
        compiler_params=_params("parallel", "arbitrary"),
        name="attn_window" if window else "attn_ctx",
    )(*args)


def _residual_out(x, y, mod, final_g):
    out = x + mod[2:3, :] * y
    if final_g is not None:
        ms = jnp.mean(out * out, axis=-1, keepdims=True)
        out = out * lax.rsqrt(ms + EPS) * final_g
    return out


def _outproj_kernel(a_ref, w_ref, x_ref, mod_ref, o_ref):
    y = jnp.dot(a_ref[0], w_ref[...], preferred_element_type=F32)
    o_ref[0] = _residual_out(x_ref[0], y, mod_ref[0], None)


def _outproj_call(a, w, x, mod, *, row_tile):
    b, n, width = a.shape
    mod_map = (lambda bi, ti: (bi, 0, 0)) if mod.shape[0] == b else (lambda bi, ti: (0, 0, 0))
    return pl.pallas_call(
        _outproj_kernel,
        grid=(b, n // row_tile),
        in_specs=[
            pl.BlockSpec((1, row_tile, width), lambda bi, ti: (bi, ti, 0)),
            _resident((width, D_MODEL)),
            pl.BlockSpec((1, row_tile, D_MODEL), lambda bi, ti: (bi, ti, 0)),
            pl.BlockSpec((1, 3, D_MODEL), mod_map),
        ],
        out_specs=pl.BlockSpec((1, row_tile, D_MODEL), lambda bi, ti: (bi, ti, 0)),
        out_shape=jax.ShapeDtypeStruct((b, n, D_MODEL), F32),
        compiler_params=_params("parallel", "parallel"),
        name="attn_outproj",
    )(a, w, x, mod)


def _conv_inproj_kernel(x_ref, mod_ref, g_ref, w_ref, b_ref, u_ref, z_ref):
    hb = _modulated_norm(x_ref[0], g_ref[...], mod_ref[0]).astype(BF16)
    for j in range(CONV_WIDTH // COL_CHUNK):
        ca = slice(j * COL_CHUNK, (j + 1) * COL_CHUNK)
        cg = slice(CONV_WIDTH + j * COL_CHUNK, CONV_WIDTH + (j + 1) * COL_CHUNK)
        cz = slice(2 * CONV_WIDTH + j * COL_CHUNK, 2 * CONV_WIDTH + (j + 1) * COL_CHUNK)
        a = jnp.dot(hb, w_ref[:, ca], preferred_element_type=F32) + b_ref[:, ca]
        gl = jnp.dot(hb, w_ref[:, cg], preferred_element_type=F32) + b_ref[:, cg]
        u_ref[0, :, ca] = (a * jax.nn.sigmoid(gl)).astype(BF16)
        z = jnp.dot(hb, w_ref[:, cz], preferred_element_type=F32) + b_ref[:, cz]
        z_ref[0, :, ca] = _silu(z).astype(BF16)


def _conv_inproj_call(x, mod, g_row, w, bias, *, row_tile):
    b, n, _ = x.shape
    mod_map = (lambda bi, ti: (bi, 0, 0)) if mod.shape[0] == b else (lambda bi, ti: (0, 0, 0))
    out_spec = pl.BlockSpec((1, row_tile, CONV_WIDTH), lambda bi, ti: (bi, ti, 0))
    out_sds = jax.ShapeDtypeStruct((b, n, CONV_WIDTH), BF16)
    return pl.pallas_call(
        _conv_inproj_kernel,
        grid=(b, n // row_tile),
        in_specs=[
            pl.BlockSpec((1, row_tile, D_MODEL), lambda bi, ti: (bi, ti, 0)),
            pl.BlockSpec((1, 3, D_MODEL), mod_map),
            _resident((1, D_MODEL)),
            _resident((D_MODEL, 3 * CONV_WIDTH)),
            _resident((1, 3 * CONV_WIDTH)),
        ],
        out_specs=[out_spec, out_spec],
        out_shape=[out_sds, out_sds],
        compiler_params=_params("parallel", "parallel"),
        name="conv_inproj",
    )(x, mod, g_row, w, bias)


def _conv_mix_kernel(up_ref, um_ref, un_ref, z_ref, dww_ref, dwb_ref, lng_ref, lnb_ref, w_ref,
                     x_ref, mod_ref, fg_ref, o_ref, win_ref, act_ref, *, row_tile, final):
    ti = pl.program_id(1)
    n_tiles = pl.num_programs(1)
    prev = jnp.where(ti > 0, up_ref[0].astype(F32), 0.0)
    nxt = jnp.where(ti < n_tiles - 1, un_ref[0].astype(F32), 0.0)
    win_ref[0:CONV_HALO, :] = prev
    win_ref[CONV_HALO:CONV_HALO + row_tile, :] = um_ref[0].astype(F32)
    win_ref[CONV_HALO + row_tile:, :] = nxt

    first = CONV_HALO - CONV_HALF

    def lane_chunk(ci, carry):
        lanes = pl.ds(pl.multiple_of(ci * V7X_LANES, V7X_LANES), V7X_LANES)
        for r0 in range(0, row_tile, CONV_ROW_CHUNK):
            acc = jnp.zeros((CONV_ROW_CHUNK, V7X_LANES), F32)
            for k in range(CONV_K):
                s = r0 + first + k
                acc = acc + win_ref[s:s + CONV_ROW_CHUNK, lanes] * dww_ref[k:k + 1, lanes]
            act_ref[r0:r0 + CONV_ROW_CHUNK, lanes] = acc + dwb_ref[:, lanes]
        return carry

    lax.fori_loop(0, CONV_WIDTH // V7X_LANES, lane_chunk, 0)

    u = act_ref[...]
    mu = jnp.mean(u, axis=-1, keepdims=True)
    uc = u - mu
    var = jnp.mean(uc * uc, axis=-1, keepdims=True)
    t = uc * lax.rsqrt(var + EPS) * lng_ref[...] + lnb_ref[...]
    a = (_silu(t) * z_ref[0].astype(F32)).astype(BF16)
    y = jnp.dot(a, w_ref[...], preferred_element_type=F32)
    o_ref[0] = _residual_out(x_ref[0], y, mod_ref[0], fg_ref[...] if final else None)


def _conv_mix_call(u, z, dw_w, dw_b, ln_g, ln_b, w_out, x, mod, final_g, *, row_tile, final):
    b, n, _ = u.shape
    assert n % row_tile == 0 and row_tile % CONV_HALO == 0 and row_tile % CONV_ROW_CHUNK == 0
    halo_per_tile = row_tile // CONV_HALO
    n_halo_blocks = n // CONV_HALO
    mod_map = (lambda bi, ti: (bi, 0, 0)) if mod.shape[0] == b else (lambda bi, ti: (0, 0, 0))
    return pl.pallas_call(
        functools.partial(_conv_mix_kernel, row_tile=row_tile, final=final),
        grid=(b, n // row_tile),
        in_specs=[
            pl.BlockSpec((1, CONV_HALO, CONV_WIDTH),
                         lambda bi, ti: (bi, jnp.maximum(ti * halo_per_tile - 1, 0), 0)),
            pl.BlockSpec((1, row_tile, CONV_WIDTH), lambda bi, ti: (bi, ti, 0)),
            pl.BlockSpec((1, CONV_HALO, CONV_WIDTH),
                         lambda bi, ti: (bi, jnp.minimum((ti + 1) * halo_per_tile, n_halo_blocks - 1), 0)),
            pl.BlockSpec((1, row_tile, CONV_WIDTH), lambda bi, ti: (bi, ti, 0)),
            _resident((CONV_K, CONV_WIDTH)),
            _resident((1, CONV_WIDTH)),
            _resident((1, CONV_WIDTH)),
            _resident((1, CONV_WIDTH)),
            _resident((CONV_WIDTH, D_MODEL)),
            pl.BlockSpec((1, row_tile, D_MODEL), lambda bi, ti: (bi, ti, 0)),
            pl.BlockSpec((1, 3, D_MODEL), mod_map),
            _resident((1, D_MODEL)),
        ],
        out_specs=pl.BlockSpec((1, row_tile, D_MODEL), lambda bi, ti: (bi, ti, 0)),
        out_shape=jax.ShapeDtypeStruct((b, n, D_MODEL), F32),
        scratch_shapes=[
            pltpu.VMEM((row_tile + 2 * CONV_HALO, CONV_WIDTH), F32),
            pltpu.VMEM((row_tile, CONV_WIDTH), F32),
        ],
        compiler_params=_params("parallel", "arbitrary"),
        name="conv_mix",
    )(u, u, u, z, dw_w, dw_b, ln_g, ln_b, w_out, x, mod, final_g)


def _rope_tables(s):
    rows = s // GRID_W
    row = jnp.repeat(jnp.arange(rows), GRID_W).astype(F32)
    col = jnp.tile(jnp.arange(GRID_W), rows).astype(F32)
    n_axis = HEAD_DIM // 4
    inv = ROPE_BASE ** (-jnp.arange(n_axis, dtype=F32) / n_axis)
    ang = jnp.concatenate([row[:, None] * inv, col[:, None] * inv], axis=-1)
    cos, sin = jnp.cos(ang), jnp.sin(ang)
    return jnp.concatenate([cos, cos], axis=-1), jnp.concatenate([-sin, sin], axis=-1)


def _attn_weight(w_in):
    kv_end = ATTN_WIDTH + 2 * KV_WIDTH
    return jnp.concatenate([w_in[:, :ATTN_WIDTH], w_in[:, kv_end:], w_in[:, ATTN_WIDTH:kv_end]],
                           axis=1).astype(BF16)


def kernel(x, c, ctx, c_ctx, ada_w, ada_b, norm_g, attn_w_in, attn_sink, attn_w_out,
           conv_w_in, conv_b_in, conv_dw_w, conv_dw_b, conv_ln_g, conv_ln_b, conv_w_out, final_g):
    b, s, _ = x.shape
    n_ctx = ctx.shape[1]
    assert DEPTH % 2 == 0
    cos, sin_signed = _rope_tables(s)
    ones_ctx = jnp.ones((n_ctx, HEAD_DIM), F32)
    zeros_ctx = jnp.zeros((n_ctx, HEAD_DIM), F32)

    n_rows = -(-(b + 1) // V7X_SUBLANES) * V7X_SUBLANES
    c_rows = jnp.concatenate([c, c_ctx[None, :], jnp.zeros((n_rows - b - 1, D_MODEL), F32)], axis=0)
    mods = _ada_call(c_rows, ada_w, ada_b)
    mods = mods.reshape(DEPTH, n_rows, 3, D_MODEL)

    final_row = final_g.reshape(1, D_MODEL)
    attn_kinds = ("q",) * 4 + ("z",) * 4 + ("k", "v")
    ctx_stream = ctx
    for i in range(DEPTH):
        kind, j = i % 2, i // 2
        ctx_out = any(l % 2 == 0 for l in range(i + 1, DEPTH))
        mod_lat = mods[i, :b]
        mod_ctx = mods[i, b:b + 1]
        g_row = norm_g[i].reshape(1, D_MODEL)
        if kind == 0:
            w_in = _attn_weight(attn_w_in[j])
            w_out = attn_w_out[j].astype(BF16)
            proj = _attn_inproj_call(x, mod_lat, g_row, w_in, cos, sin_signed,
                                     kinds=attn_kinds, rope=True, row_tile=ROW_TILE)
            if ctx_out:
                proj_c = _attn_inproj_call(ctx_stream, mod_ctx, g_row, w_in, ones_ctx, zeros_ctx,
                                           kinds=attn_kinds, rope=False, row_tile=n_ctx)
                k_col, v_col = K_COL, V_COL
                o_c = _attn_call(attn_sink[j], proj_c, proj_c, window=False, ctx_k_col=k_col, ctx_v_col=v_col)
                ctx_next = _outproj_call(o_c, w_out, ctx_stream, mod_ctx, row_tile=n_ctx)
            else:
                proj_c = _attn_inproj_call(ctx_stream, mod_ctx, g_row, w_in[:, K_COL:],
                                           ones_ctx, zeros_ctx, kinds=("k", "v"), rope=False, row_tile=n_ctx)
                k_col, v_col = 0, KV_WIDTH
                ctx_next = None
            o = _attn_call(attn_sink[j], proj, proj_c, window=True, ctx_k_col=k_col, ctx_v_col=v_col)
            x = _outproj_call(o, w_out, x, mod_lat, row_tile=ROW_TILE)
        else:
            w_in = conv_w_in[j].astype(BF16)
            w_out = conv_w_out[j].astype(BF16)
            bias = conv_b_in[j].reshape(1, 3 * CONV_WIDTH)
            conv_args = (conv_dw_w[j], conv_dw_b[j].reshape(1, CONV_WIDTH),
                         conv_ln_g[j].reshape(1, CONV_WIDTH), conv_ln_b[j].reshape(1, CONV_WIDTH), w_out)
            final = i == DEPTH - 1
            if ctx_out:
                u_c, z_c = _conv_inproj_call(ctx_stream, mod_ctx, g_row, w_in, bias, row_tile=n_ctx)
                ctx_next = _conv_mix_call(u_c, z_c, *conv_args, ctx_stream, mod_ctx, final_row,
                                          row_tile=n_ctx, final=False)
            else:
                ctx_next = None
            u, z = _conv_inproj_call(x, mod_lat, g_row, w_in, bias, row_tile=ROW_TILE)
            x = _conv_mix_call(u, z, *conv_args, x, mod_lat, final_row, row_tile=ROW_TILE, final=final)
        if ctx_next is not None:
            ctx_stream = ctx_next
    return x
```

```python
import functools
import math

import jax
import jax.numpy as jnp
from jax import lax
from jax.experimental import pallas as pl
from jax.experimental.pallas import tpu as pltpu

D_MODEL = 1024
DEPTH = 4
GRID_W = 64
N_Q_HEADS = 16
N_KV_HEADS = 4
GROUP = N_Q_HEADS // N_KV_HEADS
HEAD_DIM = 128
ATTN_WIDTH = N_Q_HEADS * HEAD_DIM
KV_WIDTH = N_KV_HEADS * HEAD_DIM
ATTN_PROJ = 2 * ATTN_WIDTH + 2 * KV_WIDTH
WINDOW = 128
ROPE_BASE = 10000.0
CONV_WIDTH = 2 * D_MODEL
CONV_K = 31
CONV_HALF = CONV_K // 2
EPS = 1e-6
LOG2E = math.log2(math.e)

V7X_LANES = 128
V7X_SUBLANES = 8
V7X_VMEM_LIMIT_BYTES = 56 * 1024 * 1024

ROW_TILE = 512
COL_CHUNK = 512
Q_TILE = 2 * WINDOW
CONV_HALO = 16
CONV_ROW_CHUNK = 64
N_LANE_CHUNKS = CONV_WIDTH // V7X_LANES

Q_COL, Z_COL, K_COL, V_COL = 0, ATTN_WIDTH, 2 * ATTN_WIDTH, 2 * ATTN_WIDTH + KV_WIDTH

BF16 = jnp.bfloat16
F32 = jnp.float32


def _params(*sem):
    return pltpu.CompilerParams(dimension_semantics=sem, vmem_limit_bytes=V7X_VMEM_LIMIT_BYTES)


def _silu(t):
    return t * jax.nn.sigmoid(t)


def _resident(shape):
    nd = len(shape)
    return pl.BlockSpec(shape, lambda *_: (0,) * nd, pipeline_mode=pl.Buffered(1))


def _ada_kernel(c_ref, w_ref, b_ref, o_ref):
    a = _silu(c_ref[...])
    o_ref[0] = jnp.dot(a.astype(BF16), w_ref[0].astype(BF16), preferred_element_type=F32) + b_ref[0]


def _ada_call(c_rows, ada_w, ada_b):
    n_rows = c_rows.shape[0]
    tn = D_MODEL
    return pl.pallas_call(
        _ada_kernel,
        grid=(DEPTH, 3 * D_MODEL // tn),
        in_specs=[
            pl.BlockSpec((n_rows, D_MODEL), lambda i, j: (0, 0)),
            pl.BlockSpec((1, D_MODEL, tn), lambda i, j: (i, 0, j)),
            pl.BlockSpec((1, 1, tn), lambda i, j: (i, 0, j)),
        ],
        out_specs=pl.BlockSpec((1, n_rows, tn), lambda i, j: (i, 0, j)),
        out_shape=jax.ShapeDtypeStruct((DEPTH, n_rows, 3 * D_MODEL), F32),
        compiler_params=_params("arbitrary", "arbitrary"),
        name="ada_mod",
    )(c_rows, ada_w, ada_b.reshape(DEPTH, 1, 3 * D_MODEL))


def _modulated_norm(x, g_row, mod):
    ms = jnp.mean(x * x, axis=-1, keepdims=True)
    y = x * lax.rsqrt(ms + EPS) * g_row
    return y * (1.0 + mod[1:2, :]) + mod[0:1, :]


def _rope(t, cos, sin_signed):
    heads = []
    for h in range(t.shape[1] // HEAD_DIM):
        th = t[:, h * HEAD_DIM:(h + 1) * HEAD_DIM]
        heads.append(th * cos + pltpu.roll(th, HEAD_DIM // 2, 1) * sin_signed)
    return jnp.concatenate(heads, axis=1)


def _attn_inproj_kernel(x_ref, mod_ref, g_ref, w_ref, cos_ref, sin_ref, o_ref, *, kinds, rope):
    hb = _modulated_norm(x_ref[0], g_ref[...], mod_ref[0]).astype(BF16)
    for j, kind in enumerate(kinds):
        cols = slice(j * COL_CHUNK, (j + 1) * COL_CHUNK)
        t = jnp.dot(hb, w_ref[:, cols], preferred_element_type=F32)
        if rope and kind in ("q", "k"):
            t = _rope(t, cos_ref[...], sin_ref[...])
        if kind == "q":
            t = t * (HEAD_DIM ** -0.5 * LOG2E)
        if kind == "z":
            t = _silu(t)
        o_ref[0, :, cols] = t.astype(BF16)


def _attn_inproj_call(x, mod, g_row, w, cos, sin_signed, *, kinds, rope, row_tile):
    b, n, _ = x.shape
    width = len(kinds) * COL_CHUNK
    assert w.shape == (D_MODEL, width) and n % row_tile == 0
    mod_map = (lambda bi, ti: (bi, 0, 0)) if mod.shape[0] == b else (lambda bi, ti: (0, 0, 0))
    return pl.pallas_call(
        functools.partial(_attn_inproj_kernel, kinds=kinds, rope=rope),
        grid=(b, n // row_tile),
        in_specs=[
            pl.BlockSpec((1, row_tile, D_MODEL), lambda bi, ti: (bi, ti, 0)),
            pl.BlockSpec((1, 3, D_MODEL), mod_map),
            _resident((1, D_MODEL)),
            _resident((D_MODEL, width)),
            pl.BlockSpec((row_tile, HEAD_DIM), lambda bi, ti: (ti, 0)),
            pl.BlockSpec((row_tile, HEAD_DIM), lambda bi, ti: (ti, 0)),
        ],
        out_specs=pl.BlockSpec((1, row_tile, width), lambda bi, ti: (bi, ti, 0)),
        out_shape=jax.ShapeDtypeStruct((b, n, width), BF16),
        compiler_params=_params("parallel", "parallel"),
        name="attn_inproj",
    )(x, mod, g_row, w, cos, sin_signed)


def _dot_nt(a, b):
    return lax.dot_general(a, b, (((1,), (1,)), ((), ())), preferred_element_type=F32)


def _attn_kernel(sink_ref, q_ref, z_ref, kc_ref, vc_ref, *rest, window, q_tile, n_ctx):
    if window:
        kp_ref, km_ref, kn_ref, vp_ref, vm_ref, vn_ref, bias_ref, o_ref, k_all, v_all = rest
        for dst, parts in ((k_all, (kc_ref, kp_ref, km_ref, kn_ref)), (v_all, (vc_ref, vp_ref, vm_ref, vn_ref))):
            row = 0
            for part in parts:
                dst[row:row + part.shape[1], :] = part[0]
                row += part.shape[1]
    else:
        (o_ref,) = rest

    for hk in range(N_KV_HEADS):
        kv_cols = slice(hk * HEAD_DIM, (hk + 1) * HEAD_DIM)
        qs = jnp.concatenate(
            [q_ref[0, :, (hk * GROUP + g) * HEAD_DIM:(hk * GROUP + g + 1) * HEAD_DIM] for g in range(GROUP)],
            axis=0)
        k = k_all[:, kv_cols] if window else kc_ref[0, :, kv_cols]
        v = v_all[:, kv_cols] if window else vc_ref[0, :, kv_cols]
        s = _dot_nt(qs, k)
        probs, denoms = [], []
        for g in range(GROUP):
            sg = s[g * q_tile:(g + 1) * q_tile, :]
            sink = jnp.full((q_tile, 1), sink_ref[hk * GROUP + g] * LOG2E, F32)
            if window:
                s_ctx, s_win = sg[:, :n_ctx], sg[:, n_ctx:] + bias_ref[0]
                m = jnp.maximum(jnp.max(s_ctx, axis=-1, keepdims=True), jnp.max(s_win, axis=-1, keepdims=True))
                m = jnp.maximum(m, sink)
                p_ctx, p_win = jnp.exp2(s_ctx - m), jnp.exp2(s_win - m)
                denom = (jnp.exp2(sink - m) + jnp.sum(p_ctx, axis=-1, keepdims=True)
                         + jnp.sum(p_win, axis=-1, keepdims=True))
                p = jnp.concatenate([p_ctx.astype(BF16), p_win.astype(BF16)], axis=1)
            else:
                m = jnp.maximum(jnp.max(sg, axis=-1, keepdims=True), sink)
                p = jnp.exp2(sg - m)
                denom = jnp.exp2(sink - m) + jnp.sum(p, axis=-1, keepdims=True)
                p = p.astype(BF16)
            probs.append(p)
            denoms.append(denom)
        o = jnp.dot(jnp.concatenate(probs, axis=0), v, preferred_element_type=F32)
        for g in range(GROUP):
            cols = slice((hk * GROUP + g) * HEAD_DIM, (hk * GROUP + g + 1) * HEAD_DIM)
            gate = z_ref[0, :, cols].astype(F32)
            o_ref[0, :, cols] = (o[g * q_tile:(g + 1) * q_tile, :] / denoms[g] * gate).astype(BF16)


def _band_bias(q_tile):
    n_win = q_tile + 2 * WINDOW
    r = jnp.arange(q_tile)[:, None]
    j = jnp.arange(n_win)[None, :]
    band = jnp.abs(j - WINDOW - r) <= WINDOW
    first = band & (j >= WINDOW)
    last = band & (j < WINDOW + q_tile)
    neg = jnp.float32(-jnp.inf)
    return jnp.stack([jnp.where(msk, 0.0, neg).astype(F32) for msk in (first, band, last)])


def _attn_call(sink, proj, proj_ctx, *, window, ctx_k_col, ctx_v_col):
    b, n, _ = proj.shape
    n_ctx = proj_ctx.shape[1]
    q_tile = Q_TILE if window else n
    n_tiles = n // q_tile
    assert n % q_tile == 0 and (not window or n_tiles >= 2)
    half_per_tile = q_tile // WINDOW
    n_half = n // WINDOW
    in_specs = [
        pl.BlockSpec(memory_space=pltpu.SMEM),
        pl.BlockSpec((1, q_tile, ATTN_WIDTH), lambda bi, ti: (bi, ti, Q_COL // ATTN_WIDTH)),
        pl.BlockSpec((1, q_tile, ATTN_WIDTH), lambda bi, ti: (bi, ti, Z_COL // ATTN_WIDTH)),
        pl.BlockSpec((1, n_ctx, KV_WIDTH), lambda bi, ti: (bi, 0, ctx_k_col // KV_WIDTH)),
        pl.BlockSpec((1, n_ctx, KV_WIDTH), lambda bi, ti: (bi, 0, ctx_v_col // KV_WIDTH)),
    ]
    args = [sink, proj, proj, proj_ctx, proj_ctx]
    scratch = []
    if window:
        for col in (K_COL // KV_WIDTH, V_COL // KV_WIDTH):
            in_specs += [
                pl.BlockSpec((1, WINDOW, KV_WIDTH),
                             lambda bi, ti, col=col: (bi, jnp.maximum(ti * half_per_tile - 1, 0), col)),
                pl.BlockSpec((1, q_tile, KV_WIDTH), lambda bi, ti, col=col: (bi, ti, col)),
                pl.BlockSpec((1, WINDOW, KV_WIDTH),
                             lambda bi, ti, col=col: (bi, jnp.minimum((ti + 1) * half_per_tile, n_half - 1), col)),
            ]
            args += [proj, proj, proj]
        in_specs.append(pl.BlockSpec(
            (1, q_tile, q_tile + 2 * WINDOW),
            lambda bi, ti: (jnp.where(ti == 0, 0, jnp.where(ti == n_tiles - 1, 2, 1)), 0, 0)))
        args.append(_band_bias(q_tile))
        n_keys = n_ctx + q_tile + 2 * WINDOW
        scratch = [pltpu.VMEM((n_keys, KV_WIDTH), BF16), pltpu.VMEM((n_keys, KV_WIDTH), BF16)]
    return pl.pallas_call(
        functools.partial(_attn_kernel, window=window, q_tile=q_tile, n_ctx=n_ctx),
        grid=(b, n_tiles),
        in_specs=in_specs,
        out_specs=pl.BlockSpec((1, q_tile, ATTN_WIDTH), lambda bi, ti: (bi, ti, 0)),
        out_shape=jax.ShapeDtypeStruct((b, n, ATTN_WIDTH), BF16),
        scratch_shapes=scratch,
        compiler_params=_params("parallel", "arbitrary"),
        name="attn_window" if window else "attn_ctx",
    )(*args)


def _residual_out(x, y, mod, final_g):
    out = x + mod[2:3, :] * y
    if final_g is not None:
        ms = jnp.mean(out * out, axis=-1, keepdims=True)
        out = out * lax.rsqrt(ms + EPS) * final_g
    return out


def _outproj_kernel(a_ref, w_ref, x_ref, mod_ref, o_ref):
    y = jnp.dot(a_ref[0], w_ref[...], preferred_element_type=F32)
    o_ref[0] = _residual_out(x_ref[0], y, mod_ref[0], None)


def _outproj_call(a, w, x, mod, *, row_tile):
    b, n, width = a.shape
    mod_map = (lambda bi, ti: (bi, 0, 0)) if mod.shape[0] == b else (lambda bi, ti: (0, 0, 0))
    return pl.pallas_call(
        _outproj_kernel,
        grid=(b, n // row_tile),
        in_specs=[
            pl.BlockSpec((1, row_tile, width), lambda bi, ti: (bi, ti, 0)),
            _resident((width, D_MODEL)),
            pl.BlockSpec((1, row_tile, D_MODEL), lambda bi, ti: (bi, ti, 0)),
            pl.BlockSpec((1, 3, D_MODEL), mod_map),
        ],
        out_specs=pl.BlockSpec((1, row_tile, D_MODEL), lambda bi, ti: (bi, ti, 0)),
        out_shape=jax.ShapeDtypeStruct((b, n, D_MODEL), F32),
        compiler_params=_params("parallel", "parallel"),
        name="attn_outproj",
    )(a, w, x, mod)


def _conv_inproj_kernel(x_ref, mod_ref, g_ref, w_ref, b_ref, u_ref, z_ref):
    hb = _modulated_norm(x_ref[0], g_ref[...], mod_ref[0]).astype(BF16)
    for j in range(CONV_WIDTH // COL_CHUNK):
        ca = slice(j * COL_CHUNK, (j + 1) * COL_CHUNK)
        cg = slice(CONV_WIDTH + j * COL_CHUNK, CONV_WIDTH + (j + 1) * COL_CHUNK)
        cz = slice(2 * CONV_WIDTH + j * COL_CHUNK, 2 * CONV_WIDTH + (j + 1) * COL_CHUNK)
        a = jnp.dot(hb, w_ref[:, ca], preferred_element_type=F32) + b_ref[:, ca]
        gl = jnp.dot(hb, w_ref[:, cg], preferred_element_type=F32) + b_ref[:, cg]
        u_ref[0, :, ca] = (a * jax.nn.sigmoid(gl)).astype(BF16)
        z = jnp.dot(hb, w_ref[:, cz], preferred_element_type=F32) + b_ref[:, cz]
        z_ref[0, :, ca] = _silu(z).astype(BF16)


def _conv_inproj_call(x, mod, g_row, w, bias, *, row_tile):
    b, n, _ = x.shape
    mod_map = (lambda bi, ti: (bi, 0, 0)) if mod.shape[0] == b else (lambda bi, ti: (0, 0, 0))
    out_spec = pl.BlockSpec((1, row_tile, CONV_WIDTH), lambda bi, ti: (bi, ti, 0))
    out_sds = jax.ShapeDtypeStruct((b, n, CONV_WIDTH), BF16)
    return pl.pallas_call(
        _conv_inproj_kernel,
        grid=(b, n // row_tile),
        in_specs=[
            pl.BlockSpec((1, row_tile, D_MODEL), lambda bi, ti: (bi, ti, 0)),
            pl.BlockSpec((1, 3, D_MODEL), mod_map),
            _resident((1, D_MODEL)),
            _resident((D_MODEL, 3 * CONV_WIDTH)),
            _resident((1, 3 * CONV_WIDTH)),
        ],
        out_specs=[out_spec, out_spec],
        out_shape=[out_sds, out_sds],
        compiler_params=_params("parallel", "parallel"),
        name="conv_inproj",
    )(x, mod, g_row, w, bias)


def _conv_mix_kernel(up_ref, um_ref, un_ref, z_ref, dww_ref, dwb_ref, lng_ref, lnb_ref, w_ref,
                     x_ref, mod_ref, fg_ref, o_ref, win_ref, act_ref, *, row_tile, final):
    ti = pl.program_id(1)
    n_tiles = pl.num_programs(1)
    prev = jnp.where(ti > 0, up_ref[0].astype(F32), 0.0)
    nxt = jnp.where(ti < n_tiles - 1, un_ref[0].astype(F32), 0.0)
    for c in range(N_LANE_CHUNKS):
        lanes = slice(c * V7X_LANES, (c + 1) * V7X_LANES)
        win_ref[c, 0:CONV_HALO, :] = prev[:, lanes]
        win_ref[c, CONV_HALO:CONV_HALO + row_tile, :] = um_ref[0, :, lanes].astype(F32)
        win_ref[c, CONV_HALO + row_tile:, :] = nxt[:, lanes]

    first = CONV_HALO - CONV_HALF

    def lane_chunk(ci, carry):
        for r0 in range(0, row_tile, CONV_ROW_CHUNK):
            acc = jnp.zeros((CONV_ROW_CHUNK, V7X_LANES), F32)
            for k in range(CONV_K):
                s = r0 + first + k
                acc = acc + win_ref[ci, s:s + CONV_ROW_CHUNK, :] * dww_ref[ci, k:k + 1, :]
            act_ref[ci, r0:r0 + CONV_ROW_CHUNK, :] = acc + dwb_ref[ci]
        return carry

    lax.fori_loop(0, N_LANE_CHUNKS, lane_chunk, 0)

    inv_n = 1.0 / CONV_WIDTH
    u = act_ref[...]
    mu = jnp.sum(jnp.sum(u, axis=0), axis=-1, keepdims=True) * inv_n
    uc = u - mu[None]
    var = jnp.sum(jnp.sum(uc * uc, axis=0), axis=-1, keepdims=True) * inv_n
    rstd = lax.rsqrt(var + EPS)
    parts = []
    for c in range(N_LANE_CHUNKS):
        lanes = slice(c * V7X_LANES, (c + 1) * V7X_LANES)
        t = uc[c] * rstd * lng_ref[c] + lnb_ref[c]
        parts.append((_silu(t) * z_ref[0, :, lanes].astype(F32)).astype(BF16))
    a = jnp.concatenate(parts, axis=1)
    y = jnp.dot(a, w_ref[...], preferred_element_type=F32)
    o_ref[0] = _residual_out(x_ref[0], y, mod_ref[0], fg_ref[...] if final else None)


def _conv_mix_call(u, z, dw_w, dw_b, ln_g, ln_b, w_out, x, mod, final_g, *, row_tile, final):
    b, n, _ = u.shape
    assert n % row_tile == 0 and row_tile % CONV_HALO == 0 and row_tile % CONV_ROW_CHUNK == 0
    halo_per_tile = row_tile // CONV_HALO
    n_halo_blocks = n // CONV_HALO
    mod_map = (lambda bi, ti: (bi, 0, 0)) if mod.shape[0] == b else (lambda bi, ti: (0, 0, 0))
    return pl.pallas_call(
        functools.partial(_conv_mix_kernel, row_tile=row_tile, final=final),
        grid=(b, n // row_tile),
        in_specs=[
            pl.BlockSpec((1, CONV_HALO, CONV_WIDTH),
                         lambda bi, ti: (bi, jnp.maximum(ti * halo_per_tile - 1, 0), 0)),
            pl.BlockSpec((1, row_tile, CONV_WIDTH), lambda bi, ti: (bi, ti, 0)),
            pl.BlockSpec((1, CONV_HALO, CONV_WIDTH),
                         lambda bi, ti: (bi, jnp.minimum((ti + 1) * halo_per_tile, n_halo_blocks - 1), 0)),
            pl.BlockSpec((1, row_tile, CONV_WIDTH), lambda bi, ti: (bi, ti, 0)),
            _resident((N_LANE_CHUNKS, CONV_K, V7X_LANES)),
            _resident((N_LANE_CHUNKS, 1, V7X_LANES)),
            _resident((N_LANE_CHUNKS, 1, V7X_LANES)),
            _resident((N_LANE_CHUNKS, 1, V7X_LANES)),
            _resident((CONV_WIDTH, D_MODEL)),
            pl.BlockSpec((1, row_tile, D_MODEL), lambda bi, ti: (bi, ti, 0)),
            pl.BlockSpec((1, 3, D_MODEL), mod_map),
            _resident((1, D_MODEL)),
        ],
        out_specs=pl.BlockSpec((1, row_tile, D_MODEL), lambda bi, ti: (bi, ti, 0)),
        out_shape=jax.ShapeDtypeStruct((b, n, D_MODEL), F32),
        scratch_shapes=[
            pltpu.VMEM((N_LANE_CHUNKS, row_tile + 2 * CONV_HALO, V7X_LANES), F32),
            pltpu.VMEM((N_LANE_CHUNKS, row_tile, V7X_LANES), F32),
        ],
        compiler_params=_params("parallel", "arbitrary"),
        name="conv_mix",
    )(u, u, u, z, dw_w, dw_b, ln_g, ln_b, w_out, x, mod, final_g)


def _rope_tables(s):
    rows = s // GRID_W
    row = jnp.repeat(jnp.arange(rows), GRID_W).astype(F32)
    col = jnp.tile(jnp.arange(GRID_W), rows).astype(F32)
    n_axis = HEAD_DIM // 4
    inv = ROPE_BASE ** (-jnp.arange(n_axis, dtype=F32) / n_axis)
    ang = jnp.concatenate([row[:, None] * inv, col[:, None] * inv], axis=-1)
    cos, sin = jnp.cos(ang), jnp.sin(ang)
    return jnp.concatenate([cos, cos], axis=-1), jnp.concatenate([-sin, sin], axis=-1)


def _lane_chunked(p):
    rows = p.shape[0]
    return p.reshape(rows, N_LANE_CHUNKS, V7X_LANES).transpose(1, 0, 2)


def _attn_weight(w_in):
    kv_end = ATTN_WIDTH + 2 * KV_WIDTH
    return jnp.concatenate([w_in[:, :ATTN_WIDTH], w_in[:, kv_end:], w_in[:, ATTN_WIDTH:kv_end]],
                           axis=1).astype(BF16)


def kernel(x, c, ctx, c_ctx, ada_w, ada_b, norm_g, attn_w_in, attn_sink, attn_w_out,
           conv_w_in, conv_b_in, conv_dw_w, conv_dw_b, conv_ln_g, conv_ln_b, conv_w_out, final_g):
    b, s, _ = x.shape
    n_ctx = ctx.shape[1]
    assert DEPTH % 2 == 0
    cos, sin_signed = _rope_tables(s)
    ones_ctx = jnp.ones((n_ctx, HEAD_DIM), F32)
    zeros_ctx = jnp.zeros((n_ctx, HEAD_DIM), F32)

    n_rows = -(-(b + 1) // V7X_SUBLANES) * V7X_SUBLANES
    c_rows = jnp.concatenate([c, c_ctx[None, :], jnp.zeros((n_rows - b - 1, D_MODEL), F32)], axis=0)
    mods = _ada_call(c_rows, ada_w, ada_b)
    mods = mods.reshape(DEPTH, n_rows, 3, D_MODEL)

    final_row = final_g.reshape(1, D_MODEL)
    attn_kinds = ("q",) * 4 + ("z",) * 4 + ("k", "v")
    ctx_stream = ctx
    for i in range(DEPTH):
        kind, j = i % 2, i // 2
        ctx_out = any(l % 2 == 0 for l in range(i + 1, DEPTH))
        mod_lat = mods[i, :b]
        mod_ctx = mods[i, b:b + 1]
        g_row = norm_g[i].reshape(1, D_MODEL)
        if kind == 0:
            w_in = _attn_weight(attn_w_in[j])
            w_out = attn_w_out[j].astype(BF16)
            proj = _attn_inproj_call(x, mod_lat, g_row, w_in, cos, sin_signed,
                                     kinds=attn_kinds, rope=True, row_tile=ROW_TILE)
            if ctx_out:
                proj_c = _attn_inproj_call(ctx_stream, mod_ctx, g_row, w_in, ones_ctx, zeros_ctx,
                                           kinds=attn_kinds, rope=False, row_tile=n_ctx)
                k_col, v_col = K_COL, V_COL
                o_c = _attn_call(attn_sink[j], proj_c, proj_c, window=False, ctx_k_col=k_col, ctx_v_col=v_col)
                ctx_next = _outproj_call(o_c, w_out, ctx_stream, mod_ctx, row_tile=n_ctx)
            else:
                proj_c = _attn_inproj_call(ctx_stream, mod_ctx, g_row, w_in[:, K_COL:],
                                           ones_ctx, zeros_ctx, kinds=("k", "v"), rope=False, row_tile=n_ctx)
                k_col, v_col = 0, KV_WIDTH
                ctx_next = None
            o = _attn_call(attn_sink[j], proj, proj_c, window=True, ctx_k_col=k_col, ctx_v_col=v_col)
            x = _outproj_call(o, w_out, x, mod_lat, row_tile=ROW_TILE)
        else:
            w_in = conv_w_in[j].astype(BF16)
            w_out = conv_w_out[j].astype(BF16)
            bias = conv_b_in[j].reshape(1, 3 * CONV_WIDTH)
            conv_args = (_lane_chunked(conv_dw_w[j]), _lane_chunked(conv_dw_b[j][None, :]),
                         _lane_chunked(conv_ln_g[j][None, :]), _lane_chunked(conv_ln_b[j][None, :]), w_out)
            final = i == DEPTH - 1
            if ctx_out:
                u_c, z_c = _conv_inproj_call(ctx_stream, mod_ctx, g_row, w_in, bias, row_tile=n_ctx)
                ctx_next = _conv_mix_call(u_c, z_c, *conv_args, ctx_stream, mod_ctx, final_row,
                                          row_tile=n_ctx, final=False)
            else:
                ctx_next = None
            u, z = _conv_inproj_call(x, mod_lat, g_row, w_in, bias, row_tile=ROW_TILE)
            x = _conv_mix_call(u, z, *conv_args, x, mod_lat, final_row, row_tile=ROW_TILE, final=final)
        if ctx_next is not None:
            ctx_stream = ctx_next
    return x
```

```python
import functools
import math

import jax
import jax.numpy as jnp
from jax import lax
from jax.experimental import pallas as pl
from jax.experimental.pallas import tpu as pltpu

D_MODEL = 1024
DEPTH = 4
GRID_W = 64
N_Q_HEADS = 16
N_KV_HEADS = 4
GROUP = N_Q_HEADS // N_KV_HEADS
HEAD_DIM = 128
ATTN_WIDTH = N_Q_HEADS * HEAD_DIM
KV_WIDTH = N_KV_HEADS * HEAD_DIM
ATTN_PROJ = 2 * ATTN_WIDTH + 2 * KV_WIDTH
WINDOW = 128
ROPE_BASE = 10000.0
CONV_WIDTH = 2 * D_MODEL
CONV_K = 31
CONV_HALF = CONV_K // 2
EPS = 1e-6
LOG2E = math.log2(math.e)

V7X_LANES = 128
V7X_SUBLANES = 8
V7X_VMEM_LIMIT_BYTES = 56 * 1024 * 1024

ROW_TILE = 512
COL_CHUNK = 512
Q_TILE = 2 * WINDOW
CONV_HALO = 16
CONV_ROW_CHUNK = 64
N_LANE_CHUNKS = CONV_WIDTH // V7X_LANES
BF16_ROWS = 16

Q_COL, Z_COL, K_COL, V_COL = 0, ATTN_WIDTH, 2 * ATTN_WIDTH, 2 * ATTN_WIDTH + KV_WIDTH

BF16 = jnp.bfloat16
F32 = jnp.float32
U32 = jnp.uint32


def _params(*sem):
    return pltpu.CompilerParams(dimension_semantics=sem, vmem_limit_bytes=V7X_VMEM_LIMIT_BYTES)


def _silu(t):
    h = 0.5 * t
    return h + h * jnp.tanh(h)


def _resident(shape):
    nd = len(shape)
    return pl.BlockSpec(shape, lambda *_: (0,) * nd, pipeline_mode=pl.Buffered(1))


def _ada_kernel(c_ref, w_ref, b_ref, o_ref):
    a = _silu(c_ref[...])
    o_ref[0] = jnp.dot(a.astype(BF16), w_ref[0].astype(BF16), preferred_element_type=F32) + b_ref[0]


def _ada_call(c_rows, ada_w, ada_b):
    n_rows = c_rows.shape[0]
    tn = D_MODEL
    return pl.pallas_call(
        _ada_kernel,
        grid=(DEPTH, 3 * D_MODEL // tn),
        in_specs=[
            pl.BlockSpec((n_rows, D_MODEL), lambda i, j: (0, 0)),
            pl.BlockSpec((1, D_MODEL, tn), lambda i, j: (i, 0, j)),
            pl.BlockSpec((1, 1, tn), lambda i, j: (i, 0, j)),
        ],
        out_specs=pl.BlockSpec((1, n_rows, tn), lambda i, j: (i, 0, j)),
        out_shape=jax.ShapeDtypeStruct((DEPTH, n_rows, 3 * D_MODEL), F32),
        compiler_params=_params("arbitrary", "arbitrary"),
        name="ada_mod",
    )(c_rows, ada_w, ada_b.reshape(DEPTH, 1, 3 * D_MODEL))


def _modulated_norm(x, g_row, mod):
    ms = jnp.mean(x * x, axis=-1, keepdims=True)
    y = x * lax.rsqrt(ms + EPS) * g_row
    return y * (1.0 + mod[1:2, :]) + mod[0:1, :]


def _rope(t, cos, sin_signed):
    heads = []
    for h in range(t.shape[1] // HEAD_DIM):
        th = t[:, h * HEAD_DIM:(h + 1) * HEAD_DIM]
        heads.append(th * cos + pltpu.roll(th, HEAD_DIM // 2, 1) * sin_signed)
    return jnp.concatenate(heads, axis=1)


def _attn_inproj_kernel(x_ref, mod_ref, g_ref, w_ref, cos_ref, sin_ref, o_ref, *, kinds, rope):
    hb = _modulated_norm(x_ref[0], g_ref[...], mod_ref[0]).astype(BF16)
    for j, kind in enumerate(kinds):
        cols = slice(j * COL_CHUNK, (j + 1) * COL_CHUNK)
        t = jnp.dot(hb, w_ref[:, cols], preferred_element_type=F32)
        if rope and kind in ("q", "k"):
            t = _rope(t, cos_ref[...], sin_ref[...])
        if kind == "q":
            t = t * (HEAD_DIM ** -0.5 * LOG2E)
        if kind == "z":
            t = _silu(t)
        o_ref[0, :, cols] = t.astype(BF16)


def _attn_inproj_call(x, mod, g_row, w, cos, sin_signed, *, kinds, rope, row_tile):
    b, n, _ = x.shape
    width = len(kinds) * COL_CHUNK
    assert w.shape == (D_MODEL, width) and n % row_tile == 0
    mod_map = (lambda bi, ti: (bi, 0, 0)) if mod.shape[0] == b else (lambda bi, ti: (0, 0, 0))
    return pl.pallas_call(
        functools.partial(_attn_inproj_kernel, kinds=kinds, rope=rope),
        grid=(b, n // row_tile),
        in_specs=[
            pl.BlockSpec((1, row_tile, D_MODEL), lambda bi, ti: (bi, ti, 0)),
            pl.BlockSpec((1, 3, D_MODEL), mod_map),
            _resident((1, D_MODEL)),
            _resident((D_MODEL, width)),
            pl.BlockSpec((row_tile, HEAD_DIM), lambda bi, ti: (ti, 0)),
            pl.BlockSpec((row_tile, HEAD_DIM), lambda bi, ti: (ti, 0)),
        ],
        out_specs=pl.BlockSpec((1, row_tile, width), lambda bi, ti: (bi, ti, 0)),
        out_shape=jax.ShapeDtypeStruct((b, n, width), BF16),
        compiler_params=_params("parallel", "parallel"),
        name="attn_inproj",
    )(x, mod, g_row, w, cos, sin_signed)


def _dot_nt(a, b):
    return lax.dot_general(a, b, (((1,), (1,)), ((), ())), preferred_element_type=F32)


def _attn_kernel(sink_ref, q_ref, z_ref, kc_ref, vc_ref, *rest, window, q_tile, n_ctx):
    if window:
        kp_ref, km_ref, kn_ref, vp_ref, vm_ref, vn_ref, bias_ref, o_ref, k_all, v_all = rest
        row = 0
        for part in (kc_ref, kp_ref, km_ref, kn_ref):
            k_all[row:row + part.shape[1], :] = part[0]
            row += part.shape[1]
        lane = lax.broadcasted_iota(jnp.int32, (k_all.shape[0], HEAD_DIM), 1)
        ones_col = jnp.where(lane == 0, 1.0, 0.0).astype(BF16)
        for hk in range(N_KV_HEADS):
            row = 0
            for part in (vc_ref, vp_ref, vm_ref, vn_ref):
                v_all[row:row + part.shape[1], 2 * hk * HEAD_DIM:(2 * hk + 1) * HEAD_DIM] = (
                    part[0, :, hk * HEAD_DIM:(hk + 1) * HEAD_DIM])
                row += part.shape[1]
            v_all[:, (2 * hk + 1) * HEAD_DIM:(2 * hk + 2) * HEAD_DIM] = ones_col
    else:
        (o_ref,) = rest

    for hk in range(N_KV_HEADS):
        kv_cols = slice(hk * HEAD_DIM, (hk + 1) * HEAD_DIM)
        qs = jnp.concatenate(
            [q_ref[0, :, (hk * GROUP + g) * HEAD_DIM:(hk * GROUP + g + 1) * HEAD_DIM] for g in range(GROUP)],
            axis=0)
        k = k_all[:, kv_cols] if window else kc_ref[0, :, kv_cols]
        v = v_all[:, 2 * hk * HEAD_DIM:(2 * hk + 2) * HEAD_DIM] if window else vc_ref[0, :, kv_cols]
        s = _dot_nt(qs, k)
        probs, denoms = [], []
        for g in range(GROUP):
            sg = s[g * q_tile:(g + 1) * q_tile, :]
            sink = jnp.full((q_tile, 1), sink_ref[hk * GROUP + g] * LOG2E, F32)
            if window:
                for h0 in range(0, q_tile, WINDOW):
                    rows = slice(h0, h0 + WINDOW)
                    pieces = [sg[rows, :n_ctx]]
                    for c in range(0, q_tile + 2 * WINDOW, WINDOW):
                        lo, hi = c - WINDOW - (h0 + WINDOW - 1), c + WINDOW - 1 - WINDOW - h0
                        if hi < -WINDOW or lo > WINDOW:
                            pieces.append(None)
                        elif lo >= -WINDOW and hi <= WINDOW and c != 0 and c != q_tile + WINDOW:
                            pieces.append(sg[rows, n_ctx + c:n_ctx + c + WINDOW])
                        else:
                            pieces.append(sg[rows, n_ctx + c:n_ctx + c + WINDOW] + bias_ref[0, rows, c:c + WINDOW])
                    m = sink[rows]
                    for pc in pieces:
                        if pc is not None:
                            m = jnp.maximum(m, jnp.max(pc, axis=-1, keepdims=True))
                    ps = [jnp.zeros((WINDOW, WINDOW), BF16) if pc is None else jnp.exp2(pc - m).astype(BF16)
                          for pc in pieces]
                    probs.append(jnp.concatenate(ps, axis=1))
                    denoms.append(jnp.exp2(sink[rows] - m))
            else:
                m = jnp.maximum(jnp.max(sg, axis=-1, keepdims=True), sink)
                p = jnp.exp2(sg - m)
                denoms.append(jnp.exp2(sink - m) + jnp.sum(p, axis=-1, keepdims=True))
                probs.append(p.astype(BF16))
        o = jnp.dot(jnp.concatenate(probs, axis=0), v, preferred_element_type=F32)
        if window:
            o = o[:, :HEAD_DIM] / (jnp.concatenate(denoms, axis=0) + o[:, HEAD_DIM:HEAD_DIM + 1])
        for g in range(GROUP):
            cols = slice((hk * GROUP + g) * HEAD_DIM, (hk * GROUP + g + 1) * HEAD_DIM)
            gate = z_ref[0, :, cols].astype(F32)
            og = o[g * q_tile:(g + 1) * q_tile, :]
            if not window:
                og = og / denoms[g]
            o_ref[0, :, cols] = (og * gate).astype(BF16)


def _band_bias(q_tile):
    n_win = q_tile + 2 * WINDOW
    r = jnp.arange(q_tile)[:, None]
    j = jnp.arange(n_win)[None, :]
    band = jnp.abs(j - WINDOW - r) <= WINDOW
    first = band & (j >= WINDOW)
    last = band & (j < WINDOW + q_tile)
    neg = jnp.float32(-jnp.inf)
    return jnp.stack([jnp.where(msk, 0.0, neg).astype(F32) for msk in (first, band, last)])


def _attn_call(sink, proj, proj_ctx, *, window, ctx_k_col, ctx_v_col):
    b, n, _ = proj.shape
    n_ctx = proj_ctx.shape[1]
    q_tile = Q_TILE if window else n
    n_tiles = n // q_tile
    assert n % q_tile == 0 and (not window or n_tiles >= 2)
    half_per_tile = q_tile // WINDOW
    n_half = n // WINDOW
    in_specs = [
        pl.BlockSpec(memory_space=pltpu.SMEM),
        pl.BlockSpec((1, q_tile, ATTN_WIDTH), lambda bi, ti: (bi, ti, Q_COL // ATTN_WIDTH)),
        pl.BlockSpec((1, q_tile, ATTN_WIDTH), lambda bi, ti: (bi, ti, Z_COL // ATTN_WIDTH)),
        pl.BlockSpec((1, n_ctx, KV_WIDTH), lambda bi, ti: (bi, 0, ctx_k_col // KV_WIDTH)),
        pl.BlockSpec((1, n_ctx, KV_WIDTH), lambda bi, ti: (bi, 0, ctx_v_col // KV_WIDTH)),
    ]
    args = [sink, proj, proj, proj_ctx, proj_ctx]
    scratch = []
    if window:
        for col in (K_COL // KV_WIDTH, V_COL // KV_WIDTH):
            in_specs += [
                pl.BlockSpec((1, WINDOW, KV_WIDTH),
                             lambda bi, ti, col=col: (bi, jnp.maximum(ti * half_per_tile - 1, 0), col)),
                pl.BlockSpec((1, q_tile, KV_WIDTH), lambda bi, ti, col=col: (bi, ti, col)),
                pl.BlockSpec((1, WINDOW, KV_WIDTH),
                             lambda bi, ti, col=col: (bi, jnp.minimum((ti + 1) * half_per_tile, n_half - 1), col)),
            ]
            args += [proj, proj, proj]
        in_specs.append(pl.BlockSpec(
            (1, q_tile, q_tile + 2 * WINDOW),
            lambda bi, ti: (jnp.where(ti == 0, 0, jnp.where(ti == n_tiles - 1, 2, 1)), 0, 0)))
        args.append(_band_bias(q_tile))
        n_keys = n_ctx + q_tile + 2 * WINDOW
        scratch = [pltpu.VMEM((n_keys, KV_WIDTH), BF16), pltpu.VMEM((n_keys, 2 * KV_WIDTH), BF16)]
    return pl.pallas_call(
        functools.partial(_attn_kernel, window=window, q_tile=q_tile, n_ctx=n_ctx),
        grid=(b, n_tiles),
        in_specs=in_specs,
        out_specs=pl.BlockSpec((1, q_tile, ATTN_WIDTH), lambda bi, ti: (bi, ti, 0)),
        out_shape=jax.ShapeDtypeStruct((b, n, ATTN_WIDTH), BF16),
        scratch_shapes=scratch,
        compiler_params=_params("parallel", "arbitrary"),
        name="attn_window" if window else "attn_ctx",
    )(*args)


def _residual_out(x, y, mod, final_g):
    out = x + mod[2:3, :] * y
    if final_g is not None:
        ms = jnp.mean(out * out, axis=-1, keepdims=True)
        out = out * lax.rsqrt(ms + EPS) * final_g
    return out


def _outproj_kernel(a_ref, w_ref, x_ref, mod_ref, o_ref):
    y = jnp.dot(a_ref[0], w_ref[...], preferred_element_type=F32)
    o_ref[0] = _residual_out(x_ref[0], y, mod_ref[0], None)


def _outproj_call(a, w, x, mod, *, row_tile):
    b, n, width = a.shape
    mod_map = (lambda bi, ti: (bi, 0, 0)) if mod.shape[0] == b else (lambda bi, ti: (0, 0, 0))
    return pl.pallas_call(
        _outproj_kernel,
        grid=(b, n // row_tile),
        in_specs=[
            pl.BlockSpec((1, row_tile, width), lambda bi, ti: (bi, ti, 0)),
            _resident((width, D_MODEL)),
            pl.BlockSpec((1, row_tile, D_MODEL), lambda bi, ti: (bi, ti, 0)),
            pl.BlockSpec((1, 3, D_MODEL), mod_map),
        ],
        out_specs=pl.BlockSpec((1, row_tile, D_MODEL), lambda bi, ti: (bi, ti, 0)),
        out_shape=jax.ShapeDtypeStruct((b, n, D_MODEL), F32),
        compiler_params=_params("parallel", "parallel"),
        name="attn_outproj",
    )(a, w, x, mod)


def _conv_inproj_kernel(x_ref, mod_ref, g_ref, w_ref, b_ref, u_ref, z_ref):
    hb = _modulated_norm(x_ref[0], g_ref[...], mod_ref[0]).astype(BF16)
    for j in range(CONV_WIDTH // COL_CHUNK):
        ca = slice(j * COL_CHUNK, (j + 1) * COL_CHUNK)
        cg = slice(CONV_WIDTH + j * COL_CHUNK, CONV_WIDTH + (j + 1) * COL_CHUNK)
        cz = slice(2 * CONV_WIDTH + j * COL_CHUNK, 2 * CONV_WIDTH + (j + 1) * COL_CHUNK)
        a = jnp.dot(hb, w_ref[:, ca], preferred_element_type=F32) + b_ref[:, ca]
        gl = jnp.dot(hb, w_ref[:, cg], preferred_element_type=F32) + b_ref[:, cg]
        u_ref[0, :, ca] = (a * jax.nn.sigmoid(gl)).astype(BF16)
        z = jnp.dot(hb, w_ref[:, cz], preferred_element_type=F32) + b_ref[:, cz]
        z_ref[0, :, ca] = _silu(z).astype(BF16)


def _conv_inproj_call(x, mod, g_row, w, bias, *, row_tile):
    b, n, _ = x.shape
    mod_map = (lambda bi, ti: (bi, 0, 0)) if mod.shape[0] == b else (lambda bi, ti: (0, 0, 0))
    out_spec = pl.BlockSpec((1, row_tile, CONV_WIDTH), lambda bi, ti: (bi, ti, 0))
    out_sds = jax.ShapeDtypeStruct((b, n, CONV_WIDTH), BF16)
    return pl.pallas_call(
        _conv_inproj_kernel,
        grid=(b, n // row_tile),
        in_specs=[
            pl.BlockSpec((1, row_tile, D_MODEL), lambda bi, ti: (bi, ti, 0)),
            pl.BlockSpec((1, 3, D_MODEL), mod_map),
            _resident((1, D_MODEL)),
            _resident((D_MODEL, 3 * CONV_WIDTH)),
            _resident((1, 3 * CONV_WIDTH)),
        ],
        out_specs=[out_spec, out_spec],
        out_shape=[out_sds, out_sds],
        compiler_params=_params("parallel", "parallel"),
        name="conv_inproj",
    )(x, mod, g_row, w, bias)


def _conv_mix_kernel(up_ref, um_ref, un_ref, z_ref, dww_ref, dwb_ref, lng_ref, lnb_ref, w_ref,
                     x_ref, mod_ref, fg_ref, o_ref, even_ref, odd_ref, act_ref, *, row_tile, final):
    ti = pl.program_id(1)
    n_tiles = pl.num_programs(1)
    half = CONV_HALO // 2
    prev = jnp.where(ti > 0, up_ref[0], jnp.zeros_like(up_ref[0]))
    nxt = jnp.where(ti < n_tiles - 1, un_ref[0], jnp.zeros_like(un_ref[0]))
    for c in range(N_LANE_CHUNKS):
        lanes = slice(c * V7X_LANES, (c + 1) * V7X_LANES)
        parts = (prev[:, lanes], um_ref[0, :, lanes], nxt[:, lanes])
        row = 0
        for part in parts:
            even_ref[c, row // 2:(row + part.shape[0]) // 2, :] = pltpu.bitcast(part, U32)
            row += part.shape[0]
        n_words = row // 2
        even_ref[c, n_words:, :] = jnp.zeros((even_ref.shape[1] - n_words, V7X_LANES), U32)
        odd_ref[c] = (even_ref[c, 0:n_words, :] >> 16) | (even_ref[c, 1:n_words + 1, :] << 16)

    first = CONV_HALO - CONV_HALF
    packed_per_chunk = CONV_ROW_CHUNK // BF16_ROWS

    def lane_chunk(ci, carry):
        for r0 in range(0, row_tile, CONV_ROW_CHUNK):
            acc = jnp.zeros((CONV_ROW_CHUNK, V7X_LANES), F32)
            for k in range(CONV_K):
                s = r0 + first + k
                src = odd_ref if s % 2 else even_ref
                word = s // 2
                taps = jnp.concatenate(
                    [pltpu.bitcast(src[ci, word + half * m:word + half * (m + 1), :], BF16)
                     for m in range(packed_per_chunk)], axis=0)
                wk = jnp.concatenate([dww_ref[ci, BF16_ROWS * k:BF16_ROWS * (k + 1), :]] * packed_per_chunk, axis=0)
                acc = acc + taps.astype(F32) * wk.astype(F32)
            act_ref[ci, r0:r0 + CONV_ROW_CHUNK, :] = acc + dwb_ref[ci]
        return carry

    lax.fori_loop(0, N_LANE_CHUNKS, lane_chunk, 0)

    inv_n = 1.0 / CONV_WIDTH
    u = act_ref[...]
    mu = jnp.sum(jnp.sum(u, axis=0), axis=-1, keepdims=True) * inv_n
    uc = u - mu[None]
    var = jnp.sum(jnp.sum(uc * uc, axis=0), axis=-1, keepdims=True) * inv_n
    rstd = lax.rsqrt(var + EPS)
    parts = []
    for c in range(N_LANE_CHUNKS):
        lanes = slice(c * V7X_LANES, (c + 1) * V7X_LANES)
        t = uc[c] * rstd * lng_ref[c] + lnb_ref[c]
        parts.append((_silu(t) * z_ref[0, :, lanes].astype(F32)).astype(BF16))
    a = jnp.concatenate(parts, axis=1)
    y = jnp.dot(a, w_ref[...], preferred_element_type=F32)
    o_ref[0] = _residual_out(x_ref[0], y, mod_ref[0], fg_ref[...] if final else None)


def _conv_mix_call(u, z, dw_w, dw_b, ln_g, ln_b, w_out, x, mod, final_g, *, row_tile, final):
    b, n, _ = u.shape
    assert n % row_tile == 0 and row_tile % CONV_HALO == 0 and row_tile % CONV_ROW_CHUNK == 0
    halo_per_tile = row_tile // CONV_HALO
    n_halo_blocks = n // CONV_HALO
    mod_map = (lambda bi, ti: (bi, 0, 0)) if mod.shape[0] == b else (lambda bi, ti: (0, 0, 0))
    return pl.pallas_call(
        functools.partial(_conv_mix_kernel, row_tile=row_tile, final=final),
        grid=(b, n // row_tile),
        in_specs=[
            pl.BlockSpec((1, CONV_HALO, CONV_WIDTH),
                         lambda bi, ti: (bi, jnp.maximum(ti * halo_per_tile - 1, 0), 0)),
            pl.BlockSpec((1, row_tile, CONV_WIDTH), lambda bi, ti: (bi, ti, 0)),
            pl.BlockSpec((1, CONV_HALO, CONV_WIDTH),
                         lambda bi, ti: (bi, jnp.minimum((ti + 1) * halo_per_tile, n_halo_blocks - 1), 0)),
            pl.BlockSpec((1, row_tile, CONV_WIDTH), lambda bi, ti: (bi, ti, 0)),
            _resident((N_LANE_CHUNKS, CONV_K * BF16_ROWS, V7X_LANES)),
            _resident((N_LANE_CHUNKS, 1, V7X_LANES)),
            _resident((N_LANE_CHUNKS, 1, V7X_LANES)),
            _resident((N_LANE_CHUNKS, 1, V7X_LANES)),
            _resident((CONV_WIDTH, D_MODEL)),
            pl.BlockSpec((1, row_tile, D_MODEL), lambda bi, ti: (bi, ti, 0)),
            pl.BlockSpec((1, 3, D_MODEL), mod_map),
            _resident((1, D_MODEL)),
        ],
        out_specs=pl.BlockSpec((1, row_tile, D_MODEL), lambda bi, ti: (bi, ti, 0)),
        out_shape=jax.ShapeDtypeStruct((b, n, D_MODEL), F32),
        scratch_shapes=[
            pltpu.VMEM((N_LANE_CHUNKS, (row_tile + 2 * CONV_HALO) // 2 + V7X_SUBLANES, V7X_LANES), U32),
            pltpu.VMEM((N_LANE_CHUNKS, (row_tile + 2 * CONV_HALO) // 2, V7X_LANES), U32),
            pltpu.VMEM((N_LANE_CHUNKS, row_tile, V7X_LANES), F32),
        ],
        compiler_params=_params("parallel", "arbitrary"),
        name="conv_mix",
    )(u, u, u, z, dw_w, dw_b, ln_g, ln_b, w_out, x, mod, final_g)


def _rope_tables(s):
    rows = s // GRID_W
    row = jnp.repeat(jnp.arange(rows), GRID_W).astype(F32)
    col = jnp.tile(jnp.arange(GRID_W), rows).astype(F32)
    n_axis = HEAD_DIM // 4
    inv = ROPE_BASE ** (-jnp.arange(n_axis, dtype=F32) / n_axis)
    ang = jnp.concatenate([row[:, None] * inv, col[:, None] * inv], axis=-1)
    cos, sin = jnp.cos(ang), jnp.sin(ang)
    return jnp.concatenate([cos, cos], axis=-1), jnp.concatenate([-sin, sin], axis=-1)


def _lane_chunked(p):
    rows = p.shape[0]
    return p.reshape(rows, N_LANE_CHUNKS, V7X_LANES).transpose(1, 0, 2)


def _attn_weight(w_in):
    kv_end = ATTN_WIDTH + 2 * KV_WIDTH
    return jnp.concatenate([w_in[:, :ATTN_WIDTH], w_in[:, kv_end:], w_in[:, ATTN_WIDTH:kv_end]],
                           axis=1).astype(BF16)


def kernel(x, c, ctx, c_ctx, ada_w, ada_b, norm_g, attn_w_in, attn_sink, attn_w_out,
           conv_w_in, conv_b_in, conv_dw_w, conv_dw_b, conv_ln_g, conv_ln_b, conv_w_out, final_g):
    b, s, _ = x.shape
    n_ctx = ctx.shape[1]
    assert DEPTH % 2 == 0
    cos, sin_signed = _rope_tables(s)
    ones_ctx = jnp.ones((n_ctx, HEAD_DIM), F32)
    zeros_ctx = jnp.zeros((n_ctx, HEAD_DIM), F32)

    n_rows = -(-(b + 1) // V7X_SUBLANES) * V7X_SUBLANES
    c_rows = jnp.concatenate([c, c_ctx[None, :], jnp.zeros((n_rows - b - 1, D_MODEL), F32)], axis=0)
    mods = _ada_call(c_rows, ada_w, ada_b)
    mods = mods.reshape(DEPTH, n_rows, 3, D_MODEL)

    final_row = final_g.reshape(1, D_MODEL)
    attn_kinds = ("q",) * 4 + ("z",) * 4 + ("k", "v")
    ctx_stream = ctx
    for i in range(DEPTH):
        kind, j = i % 2, i // 2
        ctx_out = any(l % 2 == 0 for l in range(i + 1, DEPTH))
        mod_lat = mods[i, :b]
        mod_ctx = mods[i, b:b + 1]
        g_row = norm_g[i].reshape(1, D_MODEL)
        if kind == 0:
            w_in = _attn_weight(attn_w_in[j])
            w_out = attn_w_out[j].astype(BF16)
            proj = _attn_inproj_call(x, mod_lat, g_row, w_in, cos, sin_signed,
                                     kinds=attn_kinds, rope=True, row_tile=ROW_TILE)
            if ctx_out:
                proj_c = _attn_inproj_call(ctx_stream, mod_ctx, g_row, w_in, ones_ctx, zeros_ctx,
                                           kinds=attn_kinds, rope=False, row_tile=n_ctx)
                k_col, v_col = K_COL, V_COL
                o_c = _attn_call(attn_sink[j], proj_c, proj_c, window=False, ctx_k_col=k_col, ctx_v_col=v_col)
                ctx_next = _outproj_call(o_c, w_out, ctx_stream, mod_ctx, row_tile=n_ctx)
            else:
                proj_c = _attn_inproj_call(ctx_stream, mod_ctx, g_row, w_in[:, K_COL:],
                                           ones_ctx, zeros_ctx, kinds=("k", "v"), rope=False, row_tile=n_ctx)
                k_col, v_col = 0, KV_WIDTH
                ctx_next = None
            o = _attn_call(attn_sink[j], proj, proj_c, window=True, ctx_k_col=k_col, ctx_v_col=v_col)
            x = _outproj_call(o, w_out, x, mod_lat, row_tile=ROW_TILE)
        else:
            w_in = conv_w_in[j].astype(BF16)
            w_out = conv_w_out[j].astype(BF16)
            bias = conv_b_in[j].reshape(1, 3 * CONV_WIDTH)
            dw_w = _lane_chunked(jnp.repeat(conv_dw_w[j].astype(BF16), BF16_ROWS, axis=0))
            conv_args = (dw_w, _lane_chunked(conv_dw_b[j][None, :]),
                         _lane_chunked(conv_ln_g[j][None, :]), _lane_chunked(conv_ln_b[j][None, :]), w_out)
            final = i == DEPTH - 1
            if ctx_out:
                u_c, z_c = _conv_inproj_call(ctx_stream, mod_ctx, g_row, w_in, bias, row_tile=n_ctx)
                ctx_next = _conv_mix_call(u_c, z_c, *conv_args, ctx_stream, mod_ctx, final_row,
                                          row_tile=n_ctx, final=False)
            else:
                ctx_next = None
            u, z = _conv_inproj_call(x, mod_lat, g_row, w_in, bias, row_tile=ROW_TILE)
            x = _conv_mix_call(u, z, *conv_args, x, mod_lat, final_row, row_tile=ROW_TILE, final=final)
        if ctx_next is not None:
            ctx_stream = ctx_next
    return x
```

```python
import functools
import math

import jax
import jax.numpy as jnp
from jax import lax
from jax.experimental import pallas as pl
from jax.experimental.pallas import tpu as pltpu

D_MODEL = 1024
DEPTH = 4
GRID_W = 64
N_Q_HEADS = 16
N_KV_HEADS = 4
GROUP = N_Q_HEADS // N_KV_HEADS
HEAD_DIM = 128
ATTN_WIDTH = N_Q_HEADS * HEAD_DIM
KV_WIDTH = N_KV_HEADS * HEAD_DIM
ATTN_PROJ = 2 * ATTN_WIDTH + 2 * KV_WIDTH
WINDOW = 128
ROPE_BASE = 10000.0
CONV_WIDTH = 2 * D_MODEL
CONV_K = 31
CONV_HALF = CONV_K // 2
EPS = 1e-6
LOG2E = math.log2(math.e)

V7X_LANES = 128
V7X_SUBLANES = 8
V7X_VMEM_LIMIT_BYTES = 56 * 1024 * 1024

ROW_TILE = 512
INPROJ_ROW_TILE = 1024
COL_CHUNK = 512
Q_TILE = 2 * WINDOW
CONV_HALO = 16
CONV_ROW_CHUNK = 64
N_LANE_CHUNKS = CONV_WIDTH // V7X_LANES
BF16_ROWS = 16

Q_COL, Z_COL, K_COL, V_COL = 0, ATTN_WIDTH, 2 * ATTN_WIDTH, 2 * ATTN_WIDTH + KV_WIDTH

BF16 = jnp.bfloat16
F32 = jnp.float32
U32 = jnp.uint32


def _params(*sem):
    return pltpu.CompilerParams(dimension_semantics=sem, vmem_limit_bytes=V7X_VMEM_LIMIT_BYTES)


def _silu(t):
    h = 0.5 * t
    return h + h * jnp.tanh(h)


def _resident(shape):
    nd = len(shape)
    return pl.BlockSpec(shape, lambda *_: (0,) * nd, pipeline_mode=pl.Buffered(1))


def _ada_kernel(c_ref, w_ref, b_ref, o_ref):
    a = _silu(c_ref[...])
    o_ref[0] = jnp.dot(a.astype(BF16), w_ref[0].astype(BF16), preferred_element_type=F32) + b_ref[0]


def _ada_call(c_rows, ada_w, ada_b):
    n_rows = c_rows.shape[0]
    tn = D_MODEL
    return pl.pallas_call(
        _ada_kernel,
        grid=(DEPTH, 3 * D_MODEL // tn),
        in_specs=[
            pl.BlockSpec((n_rows, D_MODEL), lambda i, j: (0, 0)),
            pl.BlockSpec((1, D_MODEL, tn), lambda i, j: (i, 0, j)),
            pl.BlockSpec((1, 1, tn), lambda i, j: (i, 0, j)),
        ],
        out_specs=pl.BlockSpec((1, n_rows, tn), lambda i, j: (i, 0, j)),
        out_shape=jax.ShapeDtypeStruct((DEPTH, n_rows, 3 * D_MODEL), F32),
        compiler_params=_params("arbitrary", "arbitrary"),
        name="ada_mod",
    )(c_rows, ada_w, ada_b.reshape(DEPTH, 1, 3 * D_MODEL))


def _modulated_norm(x, g_row, mod):
    ms = jnp.mean(x * x, axis=-1, keepdims=True)
    return x * lax.rsqrt(ms + EPS) * (g_row * (1.0 + mod[1:2, :])) + mod[0:1, :]


def _rope(t, cos, sin_signed):
    heads = []
    for h in range(t.shape[1] // HEAD_DIM):
        th = t[:, h * HEAD_DIM:(h + 1) * HEAD_DIM]
        heads.append(th * cos + pltpu.roll(th, HEAD_DIM // 2, 1) * sin_signed)
    return jnp.concatenate(heads, axis=1)


def _attn_inproj_kernel(x_ref, mod_ref, g_ref, w_ref, cos_ref, sin_ref, o_ref, *, kinds, rope):
    hb = _modulated_norm(x_ref[0], g_ref[...], mod_ref[0]).astype(BF16)
    for j, kind in enumerate(kinds):
        cols = slice(j * COL_CHUNK, (j + 1) * COL_CHUNK)
        t = jnp.dot(hb, w_ref[:, cols], preferred_element_type=F32)
        if rope and kind in ("q", "k"):
            t = _rope(t, cos_ref[...], sin_ref[...])
        if kind == "q":
            t = t * (HEAD_DIM ** -0.5 * LOG2E)
        if kind == "z":
            t = _silu(t)
        o_ref[0, :, cols] = t.astype(BF16)


def _attn_inproj_call(x, mod, g_row, w, cos, sin_signed, *, kinds, rope, row_tile):
    b, n, _ = x.shape
    width = len(kinds) * COL_CHUNK
    assert w.shape == (D_MODEL, width) and n % row_tile == 0
    mod_map = (lambda bi, ti: (bi, 0, 0)) if mod.shape[0] == b else (lambda bi, ti: (0, 0, 0))
    return pl.pallas_call(
        functools.partial(_attn_inproj_kernel, kinds=kinds, rope=rope),
        grid=(b, n // row_tile),
        in_specs=[
            pl.BlockSpec((1, row_tile, D_MODEL), lambda bi, ti: (bi, ti, 0)),
            pl.BlockSpec((1, 3, D_MODEL), mod_map),
            _resident((1, D_MODEL)),
            _resident((D_MODEL, width)),
            pl.BlockSpec((row_tile, HEAD_DIM), lambda bi, ti: (ti, 0)),
            pl.BlockSpec((row_tile, HEAD_DIM), lambda bi, ti: (ti, 0)),
        ],
        out_specs=pl.BlockSpec((1, row_tile, width), lambda bi, ti: (bi, ti, 0)),
        out_shape=jax.ShapeDtypeStruct((b, n, width), BF16),
        compiler_params=_params("parallel", "parallel"),
        name="attn_inproj",
    )(x, mod, g_row, w, cos, sin_signed)


def _dot_nt(a, b):
    return lax.dot_general(a, b, (((1,), (1,)), ((), ())), preferred_element_type=F32)


def _attn_kernel(sink_ref, q_ref, z_ref, kvc_ref, *rest, window, q_tile, n_ctx):
    if window:
        kvp_ref, kvm_ref, kvn_ref, bias_ref, o_ref, k_all, v_all = rest
        kv_parts = (kvc_ref, kvp_ref, kvm_ref, kvn_ref)
        row = 0
        for part in kv_parts:
            k_all[row:row + part.shape[1], :] = part[0, :, :KV_WIDTH]
            row += part.shape[1]
        lane = lax.broadcasted_iota(jnp.int32, (k_all.shape[0], HEAD_DIM), 1)
        ones_col = jnp.where(lane == 0, 1.0, 0.0).astype(BF16)
        for hk in range(N_KV_HEADS):
            row = 0
            for part in kv_parts:
                v_all[row:row + part.shape[1], 2 * hk * HEAD_DIM:(2 * hk + 1) * HEAD_DIM] = (
                    part[0, :, KV_WIDTH + hk * HEAD_DIM:KV_WIDTH + (hk + 1) * HEAD_DIM])
                row += part.shape[1]
            v_all[:, (2 * hk + 1) * HEAD_DIM:(2 * hk + 2) * HEAD_DIM] = ones_col
    else:
        (o_ref,) = rest

    for hk in range(N_KV_HEADS):
        kv_cols = slice(hk * HEAD_DIM, (hk + 1) * HEAD_DIM)
        qs = jnp.concatenate(
            [q_ref[0, :, (hk * GROUP + g) * HEAD_DIM:(hk * GROUP + g + 1) * HEAD_DIM] for g in range(GROUP)],
            axis=0)
        k = k_all[:, kv_cols] if window else kvc_ref[0, :, kv_cols]
        v = (v_all[:, 2 * hk * HEAD_DIM:(2 * hk + 2) * HEAD_DIM] if window
             else kvc_ref[0, :, KV_WIDTH + hk * HEAD_DIM:KV_WIDTH + (hk + 1) * HEAD_DIM])
        s = _dot_nt(qs, k)
        probs, denoms = [], []
        for g in range(GROUP):
            sg = s[g * q_tile:(g + 1) * q_tile, :]
            sink = jnp.full((q_tile, 1), sink_ref[hk * GROUP + g] * LOG2E, F32)
            if window:
                for h0 in range(0, q_tile, WINDOW):
                    rows = slice(h0, h0 + WINDOW)
                    pieces = [sg[rows, :n_ctx]]
                    for c in range(0, q_tile + 2 * WINDOW, WINDOW):
                        lo, hi = c - WINDOW - (h0 + WINDOW - 1), c + WINDOW - 1 - WINDOW - h0
                        if hi < -WINDOW or lo > WINDOW:
                            pieces.append(None)
                        elif lo >= -WINDOW and hi <= WINDOW and c != 0 and c != q_tile + WINDOW:
                            pieces.append(sg[rows, n_ctx + c:n_ctx + c + WINDOW])
                        else:
                            pieces.append(sg[rows, n_ctx + c:n_ctx + c + WINDOW] + bias_ref[0, rows, c:c + WINDOW])
                    m = sink[rows]
                    for pc in pieces:
                        if pc is not None:
                            m = jnp.maximum(m, jnp.max(pc, axis=-1, keepdims=True))
                    ps = [jnp.zeros((WINDOW, WINDOW), BF16) if pc is None else jnp.exp2(pc - m).astype(BF16)
                          for pc in pieces]
                    probs.append(jnp.concatenate(ps, axis=1))
                    denoms.append(jnp.exp2(sink[rows] - m))
            else:
                m = jnp.maximum(jnp.max(sg, axis=-1, keepdims=True), sink)
                p = jnp.exp2(sg - m)
                denoms.append(jnp.exp2(sink - m) + jnp.sum(p, axis=-1, keepdims=True))
                probs.append(p.astype(BF16))
        o = jnp.dot(jnp.concatenate(probs, axis=0), v, preferred_element_type=F32)
        if window:
            o = o[:, :HEAD_DIM] / (jnp.concatenate(denoms, axis=0) + o[:, HEAD_DIM:HEAD_DIM + 1])
        for g in range(GROUP):
            cols = slice((hk * GROUP + g) * HEAD_DIM, (hk * GROUP + g + 1) * HEAD_DIM)
            gate = z_ref[0, :, cols].astype(F32)
            og = o[g * q_tile:(g + 1) * q_tile, :]
            if not window:
                og = og / denoms[g]
            o_ref[0, :, cols] = (og * gate).astype(BF16)


def _band_bias(q_tile):
    n_win = q_tile + 2 * WINDOW
    r = jnp.arange(q_tile)[:, None]
    j = jnp.arange(n_win)[None, :]
    band = jnp.abs(j - WINDOW - r) <= WINDOW
    first = band & (j >= WINDOW)
    last = band & (j < WINDOW + q_tile)
    neg = jnp.float32(-jnp.inf)
    return jnp.stack([jnp.where(msk, 0.0, neg).astype(F32) for msk in (first, band, last)])


def _attn_call(sink, proj, proj_ctx, *, window, ctx_kv_col):
    b, n, _ = proj.shape
    n_ctx = proj_ctx.shape[1]
    q_tile = Q_TILE if window else n
    n_tiles = n // q_tile
    assert n % q_tile == 0 and (not window or n_tiles >= 2)
    half_per_tile = q_tile // WINDOW
    n_half = n // WINDOW
    kv_width = 2 * KV_WIDTH
    kv_col = K_COL // kv_width
    in_specs = [
        pl.BlockSpec(memory_space=pltpu.SMEM),
        pl.BlockSpec((1, q_tile, ATTN_WIDTH), lambda bi, ti: (bi, ti, Q_COL // ATTN_WIDTH)),
        pl.BlockSpec((1, q_tile, ATTN_WIDTH), lambda bi, ti: (bi, ti, Z_COL // ATTN_WIDTH)),
        pl.BlockSpec((1, n_ctx, kv_width), lambda bi, ti: (bi, 0, ctx_kv_col // kv_width)),
    ]
    args = [sink, proj, proj, proj_ctx]
    scratch = []
    if window:
        in_specs += [
            pl.BlockSpec((1, WINDOW, kv_width), lambda bi, ti: (bi, jnp.maximum(ti * half_per_tile - 1, 0), kv_col)),
            pl.BlockSpec((1, q_tile, kv_width), lambda bi, ti: (bi, ti, kv_col)),
            pl.BlockSpec((1, WINDOW, kv_width),
                         lambda bi, ti: (bi, jnp.minimum((ti + 1) * half_per_tile, n_half - 1), kv_col)),
        ]
        args += [proj, proj, proj]
        in_specs.append(pl.BlockSpec(
            (1, q_tile, q_tile + 2 * WINDOW),
            lambda bi, ti: (jnp.where(ti == 0, 0, jnp.where(ti == n_tiles - 1, 2, 1)), 0, 0)))
        args.append(_band_bias(q_tile))
        n_keys = n_ctx + q_tile + 2 * WINDOW
        scratch = [pltpu.VMEM((n_keys, KV_WIDTH), BF16), pltpu.VMEM((n_keys, 2 * KV_WIDTH), BF16)]
    return pl.pallas_call(
        functools.partial(_attn_kernel, window=window, q_tile=q_tile, n_ctx=n_ctx),
        grid=(b, n_tiles),
        in_specs=in_specs,
        out_specs=pl.BlockSpec((1, q_tile, ATTN_WIDTH), lambda bi, ti: (bi, ti, 0)),
        out_shape=jax.ShapeDtypeStruct((b, n, ATTN_WIDTH), BF16),
        scratch_shapes=scratch,
        compiler_params=_params("parallel", "arbitrary"),
        name="attn_window" if window else "attn_ctx",
    )(*args)


def _residual_out(x, y, mod, final_g):
    out = x + mod[2:3, :] * y
    if final_g is not None:
        ms = jnp.mean(out * out, axis=-1, keepdims=True)
        out = out * lax.rsqrt(ms + EPS) * final_g
    return out


def _outproj_kernel(a_ref, w_ref, x_ref, mod_ref, o_ref):
    y = jnp.dot(a_ref[0], w_ref[...], preferred_element_type=F32)
    o_ref[0] = _residual_out(x_ref[0], y, mod_ref[0], None)


def _outproj_call(a, w, x, mod, *, row_tile):
    b, n, width = a.shape
    mod_map = (lambda bi, ti: (bi, 0, 0)) if mod.shape[0] == b else (lambda bi, ti: (0, 0, 0))
    return pl.pallas_call(
        _outproj_kernel,
        grid=(b, n // row_tile),
        in_specs=[
            pl.BlockSpec((1, row_tile, width), lambda bi, ti: (bi, ti, 0)),
            _resident((width, D_MODEL)),
            pl.BlockSpec((1, row_tile, D_MODEL), lambda bi, ti: (bi, ti, 0)),
            pl.BlockSpec((1, 3, D_MODEL), mod_map),
        ],
        out_specs=pl.BlockSpec((1, row_tile, D_MODEL), lambda bi, ti: (bi, ti, 0)),
        out_shape=jax.ShapeDtypeStruct((b, n, D_MODEL), F32),
        compiler_params=_params("parallel", "parallel"),
        name="attn_outproj",
    )(a, w, x, mod)


def _conv_inproj_kernel(x_ref, mod_ref, g_ref, w_ref, b_ref, u_ref, z_ref):
    hb = _modulated_norm(x_ref[0], g_ref[...], mod_ref[0]).astype(BF16)
    for j in range(CONV_WIDTH // COL_CHUNK):
        ca = slice(j * COL_CHUNK, (j + 1) * COL_CHUNK)
        cg = slice(CONV_WIDTH + j * COL_CHUNK, CONV_WIDTH + (j + 1) * COL_CHUNK)
        cz = slice(2 * CONV_WIDTH + j * COL_CHUNK, 2 * CONV_WIDTH + (j + 1) * COL_CHUNK)
        a = jnp.dot(hb, w_ref[:, ca], preferred_element_type=F32) + b_ref[:, ca]
        gl = jnp.dot(hb, w_ref[:, cg], preferred_element_type=F32) + b_ref[:, cg]
        u_ref[0, :, ca] = (a * jax.nn.sigmoid(gl)).astype(BF16)
        z = jnp.dot(hb, w_ref[:, cz], preferred_element_type=F32) + b_ref[:, cz]
        z_ref[0, :, ca] = _silu(z).astype(BF16)


def _conv_inproj_call(x, mod, g_row, w, bias, *, row_tile):
    b, n, _ = x.shape
    mod_map = (lambda bi, ti: (bi, 0, 0)) if mod.shape[0] == b else (lambda bi, ti: (0, 0, 0))
    out_spec = pl.BlockSpec((1, row_tile, CONV_WIDTH), lambda bi, ti: (bi, ti, 0))
    out_sds = jax.ShapeDtypeStruct((b, n, CONV_WIDTH), BF16)
    return pl.pallas_call(
        _conv_inproj_kernel,
        grid=(b, n // row_tile),
        in_specs=[
            pl.BlockSpec((1, row_tile, D_MODEL), lambda bi, ti: (bi, ti, 0)),
            pl.BlockSpec((1, 3, D_MODEL), mod_map),
            _resident((1, D_MODEL)),
            _resident((D_MODEL, 3 * CONV_WIDTH)),
            _resident((1, 3 * CONV_WIDTH)),
        ],
        out_specs=[out_spec, out_spec],
        out_shape=[out_sds, out_sds],
        compiler_params=_params("parallel", "parallel"),
        name="conv_inproj",
    )(x, mod, g_row, w, bias)


def _conv_mix_kernel(up_ref, um_ref, un_ref, z_ref, dww_ref, dwb_ref, lng_ref, lnb_ref, w_ref,
                     x_ref, mod_ref, fg_ref, o_ref, even_ref, odd_ref, act_ref, *, row_tile, final):
    ti = pl.program_id(1)
    n_tiles = pl.num_programs(1)
    half = CONV_HALO // 2
    prev = jnp.where(ti > 0, up_ref[0], jnp.zeros_like(up_ref[0]))
    nxt = jnp.where(ti < n_tiles - 1, un_ref[0], jnp.zeros_like(un_ref[0]))
    for c in range(N_LANE_CHUNKS):
        lanes = slice(c * V7X_LANES, (c + 1) * V7X_LANES)
        parts = (prev[:, lanes], um_ref[0, :, lanes], nxt[:, lanes])
        row = 0
        for part in parts:
            even_ref[c, row // 2:(row + part.shape[0]) // 2, :] = pltpu.bitcast(part, U32)
            row += part.shape[0]
        n_words = row // 2
        even_ref[c, n_words:, :] = jnp.zeros((even_ref.shape[1] - n_words, V7X_LANES), U32)
        odd_ref[c] = (even_ref[c, 0:n_words, :] >> 16) | (even_ref[c, 1:n_words + 1, :] << 16)

    first = CONV_HALO - CONV_HALF
    packed_per_chunk = CONV_ROW_CHUNK // BF16_ROWS

    def lane_chunk(ci, carry):
        for r0 in range(0, row_tile, CONV_ROW_CHUNK):
            acc = jnp.zeros((CONV_ROW_CHUNK, V7X_LANES), F32)
            for k in range(CONV_K):
                s = r0 + first + k
                src = odd_ref if s % 2 else even_ref
                word = s // 2
                taps = jnp.concatenate(
                    [pltpu.bitcast(src[ci, word + half * m:word + half * (m + 1), :], BF16)
                     for m in range(packed_per_chunk)], axis=0)
                wk = jnp.concatenate([dww_ref[ci, BF16_ROWS * k:BF16_ROWS * (k + 1), :]] * packed_per_chunk, axis=0)
                acc = acc + taps.astype(F32) * wk.astype(F32)
            act_ref[ci, r0:r0 + CONV_ROW_CHUNK, :] = acc + dwb_ref[ci]
        return carry

    lax.fori_loop(0, N_LANE_CHUNKS, lane_chunk, 0)

    inv_n = 1.0 / CONV_WIDTH
    u = act_ref[...]
    mu = jnp.sum(jnp.sum(u, axis=0), axis=-1, keepdims=True) * inv_n
    uc = u - mu[None]
    var = jnp.sum(jnp.sum(uc * uc, axis=0), axis=-1, keepdims=True) * inv_n
    rstd = lax.rsqrt(var + EPS)
    parts = []
    for c in range(N_LANE_CHUNKS):
        lanes = slice(c * V7X_LANES, (c + 1) * V7X_LANES)
        t = uc[c] * rstd * lng_ref[c] + lnb_ref[c]
        parts.append((_silu(t) * z_ref[0, :, lanes].astype(F32)).astype(BF16))
    a = jnp.concatenate(parts, axis=1)
    y = jnp.dot(a, w_ref[...], preferred_element_type=F32)
    o_ref[0] = _residual_out(x_ref[0], y, mod_ref[0], fg_ref[...] if final else None)


def _conv_mix_call(u, z, dw_w, dw_b, ln_g, ln_b, w_out, x, mod, final_g, *, row_tile, final):
    b, n, _ = u.shape
    assert n % row_tile == 0 and row_tile % CONV_HALO == 0 and row_tile % CONV_ROW_CHUNK == 0
    halo_per_tile = row_tile // CONV_HALO
    n_halo_blocks = n // CONV_HALO
    mod_map = (lambda bi, ti: (bi, 0, 0)) if mod.shape[0] == b else (lambda bi, ti: (0, 0, 0))
    return pl.pallas_call(
        functools.partial(_conv_mix_kernel, row_tile=row_tile, final=final),
        grid=(b, n // row_tile),
        in_specs=[
            pl.BlockSpec((1, CONV_HALO, CONV_WIDTH),
                         lambda bi, ti: (bi, jnp.maximum(ti * halo_per_tile - 1, 0), 0)),
            pl.BlockSpec((1, row_tile, CONV_WIDTH), lambda bi, ti: (bi, ti, 0)),
            pl.BlockSpec((1, CONV_HALO, CONV_WIDTH),
                         lambda bi, ti: (bi, jnp.minimum((ti + 1) * halo_per_tile, n_halo_blocks - 1), 0)),
            pl.BlockSpec((1, row_tile, CONV_WIDTH), lambda bi, ti: (bi, ti, 0)),
            _resident((N_LANE_CHUNKS, CONV_K * BF16_ROWS, V7X_LANES)),
            _resident((N_LANE_CHUNKS, 1, V7X_LANES)),
            _resident((N_LANE_CHUNKS, 1, V7X_LANES)),
            _resident((N_LANE_CHUNKS, 1, V7X_LANES)),
            _resident((CONV_WIDTH, D_MODEL)),
            pl.BlockSpec((1, row_tile, D_MODEL), lambda bi, ti: (bi, ti, 0)),
            pl.BlockSpec((1, 3, D_MODEL), mod_map),
            _resident((1, D_MODEL)),
        ],
        out_specs=pl.BlockSpec((1, row_tile, D_MODEL), lambda bi, ti: (bi, ti, 0)),
        out_shape=jax.ShapeDtypeStruct((b, n, D_MODEL), F32),
        scratch_shapes=[
            pltpu.VMEM((N_LANE_CHUNKS, (row_tile + 2 * CONV_HALO) // 2 + V7X_SUBLANES, V7X_LANES), U32),
            pltpu.VMEM((N_LANE_CHUNKS, (row_tile + 2 * CONV_HALO) // 2, V7X_LANES), U32),
            pltpu.VMEM((N_LANE_CHUNKS, row_tile, V7X_LANES), F32),
        ],
        compiler_params=_params("parallel", "arbitrary"),
        name="conv_mix",
    )(u, u, u, z, dw_w, dw_b, ln_g, ln_b, w_out, x, mod, final_g)


def _rope_tables(s):
    rows = s // GRID_W
    row = jnp.repeat(jnp.arange(rows), GRID_W).astype(F32)
    col = jnp.tile(jnp.arange(GRID_W), rows).astype(F32)
    n_axis = HEAD_DIM // 4
    inv = ROPE_BASE ** (-jnp.arange(n_axis, dtype=F32) / n_axis)
    ang = jnp.concatenate([row[:, None] * inv, col[:, None] * inv], axis=-1)
    cos, sin = jnp.cos(ang), jnp.sin(ang)
    return jnp.concatenate([cos, cos], axis=-1), jnp.concatenate([-sin, sin], axis=-1)


def _lane_chunked(p):
    rows = p.shape[0]
    return p.reshape(rows, N_LANE_CHUNKS, V7X_LANES).transpose(1, 0, 2)


def _attn_weight(w_in):
    kv_end = ATTN_WIDTH + 2 * KV_WIDTH
    return jnp.concatenate([w_in[:, :ATTN_WIDTH], w_in[:, kv_end:], w_in[:, ATTN_WIDTH:kv_end]],
                           axis=1).astype(BF16)


def kernel(x, c, ctx, c_ctx, ada_w, ada_b, norm_g, attn_w_in, attn_sink, attn_w_out,
           conv_w_in, conv_b_in, conv_dw_w, conv_dw_b, conv_ln_g, conv_ln_b, conv_w_out, final_g):
    b, s, _ = x.shape
    n_ctx = ctx.shape[1]
    assert DEPTH % 2 == 0
    cos, sin_signed = _rope_tables(s)
    ones_ctx = jnp.ones((n_ctx, HEAD_DIM), F32)
    zeros_ctx = jnp.zeros((n_ctx, HEAD_DIM), F32)

    n_rows = -(-(b + 1) // V7X_SUBLANES) * V7X_SUBLANES
    c_rows = jnp.concatenate([c, c_ctx[None, :], jnp.zeros((n_rows - b - 1, D_MODEL), F32)], axis=0)
    mods = _ada_call(c_rows, ada_w, ada_b)
    mods = mods.reshape(DEPTH, n_rows, 3, D_MODEL)

    final_row = final_g.reshape(1, D_MODEL)
    attn_kinds = ("q",) * 4 + ("z",) * 4 + ("k", "v")
    ctx_stream = ctx
    for i in range(DEPTH):
        kind, j = i % 2, i // 2
        ctx_out = any(l % 2 == 0 for l in range(i + 1, DEPTH))
        mod_lat = mods[i, :b]
        mod_ctx = mods[i, b:b + 1]
        g_row = norm_g[i].reshape(1, D_MODEL)
        if kind == 0:
            w_in = _attn_weight(attn_w_in[j])
            w_out = attn_w_out[j].astype(BF16)
            proj = _attn_inproj_call(x, mod_lat, g_row, w_in, cos, sin_signed,
                                     kinds=attn_kinds, rope=True, row_tile=INPROJ_ROW_TILE)
            if ctx_out:
                proj_c = _attn_inproj_call(ctx_stream, mod_ctx, g_row, w_in, ones_ctx, zeros_ctx,
                                           kinds=attn_kinds, rope=False, row_tile=n_ctx)
                kv_col = K_COL
                o_c = _attn_call(attn_sink[j], proj_c, proj_c, window=False, ctx_kv_col=kv_col)
                ctx_next = _outproj_call(o_c, w_out, ctx_stream, mod_ctx, row_tile=n_ctx)
            else:
                proj_c = _attn_inproj_call(ctx_stream, mod_ctx, g_row, w_in[:, K_COL:],
                                           ones_ctx, zeros_ctx, kinds=("k", "v"), rope=False, row_tile=n_ctx)
                kv_col = 0
                ctx_next = None
            o = _attn_call(attn_sink[j], proj, proj_c, window=True, ctx_kv_col=kv_col)
            x = _outproj_call(o, w_out, x, mod_lat, row_tile=ROW_TILE)
        else:
            w_in = conv_w_in[j].astype(BF16)
            w_out = conv_w_out[j].astype(BF16)
            bias = conv_b_in[j].reshape(1, 3 * CONV_WIDTH)
            dw_w = _lane_chunked(jnp.repeat(conv_dw_w[j].astype(BF16), BF16_ROWS, axis=0))
            conv_args = (dw_w, _lane_chunked(conv_dw_b[j][None, :]),
                         _lane_chunked(conv_ln_g[j][None, :]), _lane_chunked(conv_ln_b[j][None, :]), w_out)
            final = i == DEPTH - 1
            if ctx_out:
                u_c, z_c = _conv_inproj_call(ctx_stream, mod_ctx, g_row, w_in, bias, row_tile=n_ctx)
                ctx_next = _conv_mix_call(u_c, z_c, *conv_args, ctx_stream, mod_ctx, final_row,
                                          row_tile=n_ctx, final=False)
            else:
                ctx_next = None
            u, z = _conv_inproj_call(x, mod_lat, g_row, w_in, bias, row_tile=INPROJ_ROW_TILE)
            x = _conv_mix_call(u, z, *conv_args, x, mod_lat, final_row, row_tile=ROW_TILE, final=final)
        if ctx_next is not None:
            ctx_stream = ctx_next
    return x
```

```python
import functools
import math

import jax
import jax.numpy as jnp
from jax import lax
from jax.experimental import pallas as pl
from jax.experimental.pallas import tpu as pltpu

D_MODEL = 1024
DEPTH = 4
GRID_W = 64
N_Q_HEADS = 16
N_KV_HEADS = 4
GROUP = N_Q_HEADS // N_KV_HEADS
HEAD_DIM = 128
ATTN_WIDTH = N_Q_HEADS * HEAD_DIM
KV_WIDTH = N_KV_HEADS * HEAD_DIM
ATTN_PROJ = 2 * ATTN_WIDTH + 2 * KV_WIDTH
WINDOW = 128
ROPE_BASE = 10000.0
CONV_WIDTH = 2 * D_MODEL
CONV_K = 31
CONV_HALF = CONV_K // 2
EPS = 1e-6
LOG2E = math.log2(math.e)

V7X_LANES = 128
V7X_SUBLANES = 8
V7X_VMEM_LIMIT_BYTES = 56 * 1024 * 1024

ROW_TILE = 512
INPROJ_ROW_TILE = 1024
COL_CHUNK = 512
Q_TILE = 2 * WINDOW
CONV_HALO = 16
CONV_ROW_CHUNK = 64
N_LANE_CHUNKS = CONV_WIDTH // V7X_LANES
BF16_ROWS = 16

Q_COL, Z_COL, K_COL, V_COL = 0, ATTN_WIDTH, 2 * ATTN_WIDTH, 2 * ATTN_WIDTH + KV_WIDTH

BF16 = jnp.bfloat16
F32 = jnp.float32
U32 = jnp.uint32


def _params(*sem):
    return pltpu.CompilerParams(dimension_semantics=sem, vmem_limit_bytes=V7X_VMEM_LIMIT_BYTES)


def _silu(t):
    h = 0.5 * t
    return h + h * jnp.tanh(h)


def _resident(shape):
    nd = len(shape)
    return pl.BlockSpec(shape, lambda *_: (0,) * nd, pipeline_mode=pl.Buffered(1))


def _ada_kernel(c_ref, w_ref, b_ref, o_ref):
    a = _silu(c_ref[...])
    o_ref[0] = jnp.dot(a.astype(BF16), w_ref[0].astype(BF16), preferred_element_type=F32) + b_ref[0]


def _ada_call(c_rows, ada_w, ada_b):
    n_rows = c_rows.shape[0]
    tn = D_MODEL
    return pl.pallas_call(
        _ada_kernel,
        grid=(DEPTH, 3 * D_MODEL // tn),
        in_specs=[
            pl.BlockSpec((n_rows, D_MODEL), lambda i, j: (0, 0)),
            pl.BlockSpec((1, D_MODEL, tn), lambda i, j: (i, 0, j)),
            pl.BlockSpec((1, 1, tn), lambda i, j: (i, 0, j)),
        ],
        out_specs=pl.BlockSpec((1, n_rows, tn), lambda i, j: (i, 0, j)),
        out_shape=jax.ShapeDtypeStruct((DEPTH, n_rows, 3 * D_MODEL), F32),
        compiler_params=_params("arbitrary", "arbitrary"),
        name="ada_mod",
    )(c_rows, ada_w, ada_b.reshape(DEPTH, 1, 3 * D_MODEL))


def _modulated_norm(x, g_row, mod):
    ms = jnp.mean(x * x, axis=-1, keepdims=True)
    return x * lax.rsqrt(ms + EPS) * (g_row * (1.0 + mod[1:2, :])) + mod[0:1, :]


def _rope(t, cos, sin_signed):
    heads = []
    for h in range(t.shape[1] // HEAD_DIM):
        th = t[:, h * HEAD_DIM:(h + 1) * HEAD_DIM]
        heads.append(th * cos + pltpu.roll(th, HEAD_DIM // 2, 1) * sin_signed)
    return jnp.concatenate(heads, axis=1)


def _attn_inproj_kernel(x_ref, mod_ref, g_ref, w_ref, cos_ref, sin_ref, o_ref, *, kinds, rope):
    hb = _modulated_norm(x_ref[0], g_ref[...], mod_ref[0]).astype(BF16)
    for j, kind in enumerate(kinds):
        cols = slice(j * COL_CHUNK, (j + 1) * COL_CHUNK)
        t = jnp.dot(hb, w_ref[:, cols], preferred_element_type=F32)
        if rope and kind in ("q", "k"):
            t = _rope(t, cos_ref[...], sin_ref[...])
        if kind == "q":
            t = t * (HEAD_DIM ** -0.5 * LOG2E)
        if kind == "z":
            t = _silu(t)
        o_ref[0, :, cols] = t.astype(BF16)


def _attn_inproj_call(x, mod, g_row, w, cos, sin_signed, *, kinds, rope, row_tile):
    b, n, _ = x.shape
    width = len(kinds) * COL_CHUNK
    assert w.shape == (D_MODEL, width) and n % row_tile == 0
    mod_map = (lambda bi, ti: (bi, 0, 0)) if mod.shape[0] == b else (lambda bi, ti: (0, 0, 0))
    return pl.pallas_call(
        functools.partial(_attn_inproj_kernel, kinds=kinds, rope=rope),
        grid=(b, n // row_tile),
        in_specs=[
            pl.BlockSpec((1, row_tile, D_MODEL), lambda bi, ti: (bi, ti, 0)),
            pl.BlockSpec((1, 3, D_MODEL), mod_map),
            _resident((1, D_MODEL)),
            _resident((D_MODEL, width)),
            pl.BlockSpec((row_tile, HEAD_DIM), lambda bi, ti: (ti, 0)),
            pl.BlockSpec((row_tile, HEAD_DIM), lambda bi, ti: (ti, 0)),
        ],
        out_specs=pl.BlockSpec((1, row_tile, width), lambda bi, ti: (bi, ti, 0)),
        out_shape=jax.ShapeDtypeStruct((b, n, width), BF16),
        compiler_params=_params("parallel", "parallel"),
        name="attn_inproj",
    )(x, mod, g_row, w, cos, sin_signed)


def _dot_nt(a, b):
    return lax.dot_general(a, b, (((1,), (1,)), ((), ())), preferred_element_type=F32)


def _attn_kernel(sink_ref, q_ref, z_ref, kvc_ref, *rest, window, q_tile, n_ctx):
    if window:
        kvp_ref, kvm_ref, kvn_ref, bias_ref, o_ref, k_all, v_all = rest
        kv_parts = (kvc_ref, kvp_ref, kvm_ref, kvn_ref)
        row = 0
        for part in kv_parts:
            k_all[row:row + part.shape[1], :] = part[0, :, :KV_WIDTH]
            row += part.shape[1]
        lane = lax.broadcasted_iota(jnp.int32, (k_all.shape[0], HEAD_DIM), 1)
        ones_col = jnp.where(lane == 0, 1.0, 0.0).astype(BF16)
        for hk in range(N_KV_HEADS):
            row = 0
            for part in kv_parts:
                v_all[row:row + part.shape[1], 2 * hk * HEAD_DIM:(2 * hk + 1) * HEAD_DIM] = (
                    part[0, :, KV_WIDTH + hk * HEAD_DIM:KV_WIDTH + (hk + 1) * HEAD_DIM])
                row += part.shape[1]
            v_all[:, (2 * hk + 1) * HEAD_DIM:(2 * hk + 2) * HEAD_DIM] = ones_col
    else:
        (o_ref,) = rest

    for hk in range(N_KV_HEADS):
        kv_cols = slice(hk * HEAD_DIM, (hk + 1) * HEAD_DIM)
        qs = jnp.concatenate(
            [q_ref[0, :, (hk * GROUP + g) * HEAD_DIM:(hk * GROUP + g + 1) * HEAD_DIM] for g in range(GROUP)],
            axis=0)
        k = k_all[:, kv_cols] if window else kvc_ref[0, :, kv_cols]
        v = (v_all[:, 2 * hk * HEAD_DIM:(2 * hk + 2) * HEAD_DIM] if window
             else kvc_ref[0, :, KV_WIDTH + hk * HEAD_DIM:KV_WIDTH + (hk + 1) * HEAD_DIM])
        s = _dot_nt(qs, k)
        probs, denoms = [], []
        for g in range(GROUP):
            sg = s[g * q_tile:(g + 1) * q_tile, :]
            sink = jnp.full((q_tile, 1), sink_ref[hk * GROUP + g] * LOG2E, F32)
            if window:
                for h0 in range(0, q_tile, WINDOW):
                    rows = slice(h0, h0 + WINDOW)
                    pieces = [sg[rows, :n_ctx]]
                    for c in range(0, q_tile + 2 * WINDOW, WINDOW):
                        lo, hi = c - WINDOW - (h0 + WINDOW - 1), c + WINDOW - 1 - WINDOW - h0
                        if hi < -WINDOW or lo > WINDOW:
                            pieces.append(None)
                        elif lo >= -WINDOW and hi <= WINDOW and c != 0 and c != q_tile + WINDOW:
                            pieces.append(sg[rows, n_ctx + c:n_ctx + c + WINDOW])
                        else:
                            pieces.append(sg[rows, n_ctx + c:n_ctx + c + WINDOW] + bias_ref[0, rows, c:c + WINDOW])
                    m = sink[rows]
                    for pc in pieces:
                        if pc is not None:
                            m = jnp.maximum(m, jnp.max(pc, axis=-1, keepdims=True))
                    ps = [jnp.zeros((WINDOW, WINDOW), BF16) if pc is None else jnp.exp2(pc - m).astype(BF16)
                          for pc in pieces]
                    probs.append(jnp.concatenate(ps, axis=1))
                    denoms.append(jnp.exp2(sink[rows] - m))
            else:
                m = jnp.maximum(jnp.max(sg, axis=-1, keepdims=True), sink)
                p = jnp.exp2(sg - m)
                denoms.append(jnp.exp2(sink - m) + jnp.sum(p, axis=-1, keepdims=True))
                probs.append(p.astype(BF16))
        o = jnp.dot(jnp.concatenate(probs, axis=0), v, preferred_element_type=F32)
        if window:
            o = o[:, :HEAD_DIM] / (jnp.concatenate(denoms, axis=0) + o[:, HEAD_DIM:HEAD_DIM + 1])
        for g in range(GROUP):
            cols = slice((hk * GROUP + g) * HEAD_DIM, (hk * GROUP + g + 1) * HEAD_DIM)
            gate = z_ref[0, :, cols].astype(F32)
            og = o[g * q_tile:(g + 1) * q_tile, :]
            if not window:
                og = og / denoms[g]
            o_ref[0, :, cols] = (og * gate).astype(BF16)


def _band_bias(q_tile):
    n_win = q_tile + 2 * WINDOW
    r = jnp.arange(q_tile)[:, None]
    j = jnp.arange(n_win)[None, :]
    band = jnp.abs(j - WINDOW - r) <= WINDOW
    first = band & (j >= WINDOW)
    last = band & (j < WINDOW + q_tile)
    neg = jnp.float32(-jnp.inf)
    return jnp.stack([jnp.where(msk, 0.0, neg).astype(F32) for msk in (first, band, last)])


def _attn_call(sink, proj, proj_ctx, *, window, ctx_kv_col):
    b, n, _ = proj.shape
    n_ctx = proj_ctx.shape[1]
    q_tile = Q_TILE if window else n
    n_tiles = n // q_tile
    assert n % q_tile == 0 and (not window or n_tiles >= 2)
    half_per_tile = q_tile // WINDOW
    n_half = n // WINDOW
    kv_width = 2 * KV_WIDTH
    kv_col = K_COL // kv_width
    in_specs = [
        pl.BlockSpec(memory_space=pltpu.SMEM),
        pl.BlockSpec((1, q_tile, ATTN_WIDTH), lambda bi, ti: (bi, ti, Q_COL // ATTN_WIDTH)),
        pl.BlockSpec((1, q_tile, ATTN_WIDTH), lambda bi, ti: (bi, ti, Z_COL // ATTN_WIDTH)),
        pl.BlockSpec((1, n_ctx, kv_width), lambda bi, ti: (bi, 0, ctx_kv_col // kv_width)),
    ]
    args = [sink, proj, proj, proj_ctx]
    scratch = []
    if window:
        in_specs += [
            pl.BlockSpec((1, WINDOW, kv_width), lambda bi, ti: (bi, jnp.maximum(ti * half_per_tile - 1, 0), kv_col)),
            pl.BlockSpec((1, q_tile, kv_width), lambda bi, ti: (bi, ti, kv_col)),
            pl.BlockSpec((1, WINDOW, kv_width),
                         lambda bi, ti: (bi, jnp.minimum((ti + 1) * half_per_tile, n_half - 1), kv_col)),
        ]
        args += [proj, proj, proj]
        in_specs.append(pl.BlockSpec(
            (1, q_tile, q_tile + 2 * WINDOW),
            lambda bi, ti: (jnp.where(ti == 0, 0, jnp.where(ti == n_tiles - 1, 2, 1)), 0, 0)))
        args.append(_band_bias(q_tile))
        n_keys = n_ctx + q_tile + 2 * WINDOW
        scratch = [pltpu.VMEM((n_keys, KV_WIDTH), BF16), pltpu.VMEM((n_keys, 2 * KV_WIDTH), BF16)]
    return pl.pallas_call(
        functools.partial(_attn_kernel, window=window, q_tile=q_tile, n_ctx=n_ctx),
        grid=(b, n_tiles),
        in_specs=in_specs,
        out_specs=pl.BlockSpec((1, q_tile, ATTN_WIDTH), lambda bi, ti: (bi, ti, 0)),
        out_shape=jax.ShapeDtypeStruct((b, n, ATTN_WIDTH), BF16),
        scratch_shapes=scratch,
        compiler_params=_params("parallel", "arbitrary"),
        name="attn_window" if window else "attn_ctx",
    )(*args)


def _residual_out(x, y, mod, final_g):
    out = x + mod[2:3, :] * y
    if final_g is not None:
        ms = jnp.mean(out * out, axis=-1, keepdims=True)
        out = out * lax.rsqrt(ms + EPS) * final_g
    return out


def _outproj_kernel(a_ref, w_ref, x_ref, mod_ref, o_ref):
    y = jnp.dot(a_ref[0], w_ref[...], preferred_element_type=F32)
    o_ref[0] = _residual_out(x_ref[0], y, mod_ref[0], None)


def _outproj_call(a, w, x, mod, *, row_tile):
    b, n, width = a.shape
    mod_map = (lambda bi, ti: (bi, 0, 0)) if mod.shape[0] == b else (lambda bi, ti: (0, 0, 0))
    return pl.pallas_call(
        _outproj_kernel,
        grid=(b, n // row_tile),
        in_specs=[
            pl.BlockSpec((1, row_tile, width), lambda bi, ti: (bi, ti, 0)),
            _resident((width, D_MODEL)),
            pl.BlockSpec((1, row_tile, D_MODEL), lambda bi, ti: (bi, ti, 0)),
            pl.BlockSpec((1, 3, D_MODEL), mod_map),
        ],
        out_specs=pl.BlockSpec((1, row_tile, D_MODEL), lambda bi, ti: (bi, ti, 0)),
        out_shape=jax.ShapeDtypeStruct((b, n, D_MODEL), F32),
        compiler_params=_params("parallel", "parallel"),
        name="attn_outproj",
    )(a, w, x, mod)


def _conv_inproj_kernel(x_ref, mod_ref, g_ref, w_ref, b_ref, u_ref, z_ref):
    hb = _modulated_norm(x_ref[0], g_ref[...], mod_ref[0]).astype(BF16)
    for j in range(CONV_WIDTH // COL_CHUNK):
        ca = slice(j * COL_CHUNK, (j + 1) * COL_CHUNK)
        cg = slice(CONV_WIDTH + j * COL_CHUNK, CONV_WIDTH + (j + 1) * COL_CHUNK)
        cz = slice(2 * CONV_WIDTH + j * COL_CHUNK, 2 * CONV_WIDTH + (j + 1) * COL_CHUNK)
        a = jnp.dot(hb, w_ref[:, ca], preferred_element_type=F32) + b_ref[:, ca]
        gl = jnp.dot(hb, w_ref[:, cg], preferred_element_type=F32) + b_ref[:, cg]
        u_ref[0, :, ca] = (a * jax.nn.sigmoid(gl)).astype(BF16)
        z = jnp.dot(hb, w_ref[:, cz], preferred_element_type=F32) + b_ref[:, cz]
        z_ref[0, :, ca] = _silu(z).astype(BF16)


def _conv_inproj_call(x, mod, g_row, w, bias, *, row_tile):
    b, n, _ = x.shape
    mod_map = (lambda bi, ti: (bi, 0, 0)) if mod.shape[0] == b else (lambda bi, ti: (0, 0, 0))
    out_spec = pl.BlockSpec((1, row_tile, CONV_WIDTH), lambda bi, ti: (bi, ti, 0))
    out_sds = jax.ShapeDtypeStruct((b, n, CONV_WIDTH), BF16)
    return pl.pallas_call(
        _conv_inproj_kernel,
        grid=(b, n // row_tile),
        in_specs=[
            pl.BlockSpec((1, row_tile, D_MODEL), lambda bi, ti: (bi, ti, 0)),
            pl.BlockSpec((1, 3, D_MODEL), mod_map),
            _resident((1, D_MODEL)),
            _resident((D_MODEL, 3 * CONV_WIDTH)),
            _resident((1, 3 * CONV_WIDTH)),
        ],
        out_specs=[out_spec, out_spec],
        out_shape=[out_sds, out_sds],
        compiler_params=_params("parallel", "parallel"),
        name="conv_inproj",
    )(x, mod, g_row, w, bias)


def _conv_mix_kernel(up_ref, um_ref, un_ref, z_ref, dww_ref, dwb_ref, lng_ref, lnb_ref, w_ref,
                     x_ref, mod_ref, fg_ref, o_ref, even_ref, odd_ref, act_ref, *, row_tile, final):
    ti = pl.program_id(1)
    n_tiles = pl.num_programs(1)
    half = CONV_HALO // 2
    prev = jnp.where(ti > 0, up_ref[0], jnp.zeros_like(up_ref[0]))
    nxt = jnp.where(ti < n_tiles - 1, un_ref[0], jnp.zeros_like(un_ref[0]))
    for c in range(N_LANE_CHUNKS):
        lanes = slice(c * V7X_LANES, (c + 1) * V7X_LANES)
        parts = (prev[:, lanes], um_ref[0, :, lanes], nxt[:, lanes])
        row = 0
        for part in parts:
            even_ref[c, row // 2:(row + part.shape[0]) // 2, :] = pltpu.bitcast(part, U32)
            row += part.shape[0]
        n_words = row // 2
        even_ref[c, n_words:, :] = jnp.zeros((even_ref.shape[1] - n_words, V7X_LANES), U32)
        odd_ref[c] = (even_ref[c, 0:n_words, :] >> 16) | (even_ref[c, 1:n_words + 1, :] << 16)

    first = CONV_HALO - CONV_HALF
    packed_per_chunk = CONV_ROW_CHUNK // BF16_ROWS

    def lane_chunk(ci, carry):
        for r0 in range(0, row_tile, CONV_ROW_CHUNK):
            acc = jnp.zeros((CONV_ROW_CHUNK, V7X_LANES), F32)
            for k in range(CONV_K):
                s = r0 + first + k
                src = odd_ref if s % 2 else even_ref
                word = s // 2
                taps = jnp.concatenate(
                    [pltpu.bitcast(src[ci, word + half * m:word + half * (m + 1), :], BF16)
                     for m in range(packed_per_chunk)], axis=0)
                wk = jnp.concatenate([dww_ref[ci, BF16_ROWS * k:BF16_ROWS * (k + 1), :]] * packed_per_chunk, axis=0)
                acc = acc + taps.astype(F32) * wk.astype(F32)
            act_ref[ci, r0:r0 + CONV_ROW_CHUNK, :] = acc + dwb_ref[ci]
        return carry

    lax.fori_loop(0, N_LANE_CHUNKS, lane_chunk, 0)

    inv_n = 1.0 / CONV_WIDTH
    u = act_ref[...]
    mu = jnp.sum(jnp.sum(u, axis=0), axis=-1, keepdims=True) * inv_n
    uc = u - mu[None]
    var = jnp.sum(jnp.sum(uc * uc, axis=0), axis=-1, keepdims=True) * inv_n
    rstd = lax.rsqrt(var + EPS)
    parts = []
    for c in range(N_LANE_CHUNKS):
        lanes = slice(c * V7X_LANES, (c + 1) * V7X_LANES)
        t = uc[c] * rstd * lng_ref[c] + lnb_ref[c]
        parts.append((_silu(t) * z_ref[0, :, lanes].astype(F32)).astype(BF16))
    a = jnp.concatenate(parts, axis=1)
    y = jnp.dot(a, w_ref[...], preferred_element_type=F32)
    o_ref[0] = _residual_out(x_ref[0], y, mod_ref[0], fg_ref[...] if final else None)


def _conv_mix_call(u, z, dw_w, dw_b, ln_g, ln_b, w_out, x, mod, final_g, *, row_tile, final):
    b, n, _ = u.shape
    assert n % row_tile == 0 and row_tile % CONV_HALO == 0 and row_tile % CONV_ROW_CHUNK == 0
    halo_per_tile = row_tile // CONV_HALO
    n_halo_blocks = n // CONV_HALO
    mod_map = (lambda bi, ti: (bi, 0, 0)) if mod.shape[0] == b else (lambda bi, ti: (0, 0, 0))
    return pl.pallas_call(
        functools.partial(_conv_mix_kernel, row_tile=row_tile, final=final),
        grid=(b, n // row_tile),
        in_specs=[
            pl.BlockSpec((1, CONV_HALO, CONV_WIDTH),
                         lambda bi, ti: (bi, jnp.maximum(ti * halo_per_tile - 1, 0), 0)),
            pl.BlockSpec((1, row_tile, CONV_WIDTH), lambda bi, ti: (bi, ti, 0)),
            pl.BlockSpec((1, CONV_HALO, CONV_WIDTH),
                         lambda bi, ti: (bi, jnp.minimum((ti + 1) * halo_per_tile, n_halo_blocks - 1), 0)),
            pl.BlockSpec((1, row_tile, CONV_WIDTH), lambda bi, ti: (bi, ti, 0)),
            _resident((N_LANE_CHUNKS, CONV_K * BF16_ROWS, V7X_LANES)),
            _resident((N_LANE_CHUNKS, 1, V7X_LANES)),
            _resident((N_LANE_CHUNKS, 1, V7X_LANES)),
            _resident((N_LANE_CHUNKS, 1, V7X_LANES)),
            _resident((CONV_WIDTH, D_MODEL)),
            pl.BlockSpec((1, row_tile, D_MODEL), lambda bi, ti: (bi, ti, 0)),
            pl.BlockSpec((1, 3, D_MODEL), mod_map),
            _resident((1, D_MODEL)),
        ],
        out_specs=pl.BlockSpec((1, row_tile, D_MODEL), lambda bi, ti: (bi, ti, 0)),
        out_shape=jax.ShapeDtypeStruct((b, n, D_MODEL), F32),
        scratch_shapes=[
            pltpu.VMEM((N_LANE_CHUNKS, (row_tile + 2 * CONV_HALO) // 2 + V7X_SUBLANES, V7X_LANES), U32),
            pltpu.VMEM((N_LANE_CHUNKS, (row_tile + 2 * CONV_HALO) // 2, V7X_LANES), U32),
            pltpu.VMEM((N_LANE_CHUNKS, row_tile, V7X_LANES), F32),
        ],
        compiler_params=_params("parallel", "arbitrary"),
        name="conv_mix",
    )(u, u, u, z, dw_w, dw_b, ln_g, ln_b, w_out, x, mod, final_g)


def _rope_tables(s):
    rows = s // GRID_W
    row = jnp.repeat(jnp.arange(rows), GRID_W).astype(F32)
    col = jnp.tile(jnp.arange(GRID_W), rows).astype(F32)
    n_axis = HEAD_DIM // 4
    inv = ROPE_BASE ** (-jnp.arange(n_axis, dtype=F32) / n_axis)
    ang = jnp.concatenate([row[:, None] * inv, col[:, None] * inv], axis=-1)
    cos, sin = jnp.cos(ang), jnp.sin(ang)
    return jnp.concatenate([cos, cos], axis=-1), jnp.concatenate([-sin, sin], axis=-1)


def _lane_chunked(p):
    rows = p.shape[0]
    return p.reshape(rows, N_LANE_CHUNKS, V7X_LANES).transpose(1, 0, 2)


def _attn_weight(w_in):
    kv_end = ATTN_WIDTH + 2 * KV_WIDTH
    return jnp.concatenate([w_in[:, :ATTN_WIDTH], w_in[:, kv_end:], w_in[:, ATTN_WIDTH:kv_end]],
                           axis=1).astype(BF16)


def kernel(x, c, ctx, c_ctx, ada_w, ada_b, norm_g, attn_w_in, attn_sink, attn_w_out,
           conv_w_in, conv_b_in, conv_dw_w, conv_dw_b, conv_ln_g, conv_ln_b, conv_w_out, final_g):
    b, s, _ = x.shape
    n_ctx = ctx.shape[1]
    assert DEPTH % 2 == 0
    cos, sin_signed = _rope_tables(s)
    ones_ctx = jnp.ones((ROW_TILE, HEAD_DIM), F32)
    zeros_ctx = jnp.zeros((ROW_TILE, HEAD_DIM), F32)
    assert (b * n_ctx) % ROW_TILE == 0

    def merged(t):
        return t.reshape(b * n_ctx // ROW_TILE, ROW_TILE, t.shape[-1])

    def split(t):
        return t.reshape(b, n_ctx, t.shape[-1])

    n_rows = -(-(b + 1) // V7X_SUBLANES) * V7X_SUBLANES
    c_rows = jnp.concatenate([c, c_ctx[None, :], jnp.zeros((n_rows - b - 1, D_MODEL), F32)], axis=0)
    mods = _ada_call(c_rows, ada_w, ada_b)
    mods = mods.reshape(DEPTH, n_rows, 3, D_MODEL)

    final_row = final_g.reshape(1, D_MODEL)
    attn_kinds = ("q",) * 4 + ("z",) * 4 + ("k", "v")
    ctx_stream = ctx
    for i in range(DEPTH):
        kind, j = i % 2, i // 2
        ctx_out = any(l % 2 == 0 for l in range(i + 1, DEPTH))
        mod_lat = mods[i, :b]
        mod_ctx = mods[i, b:b + 1]
        g_row = norm_g[i].reshape(1, D_MODEL)
        if kind == 0:
            w_in = _attn_weight(attn_w_in[j])
            w_out = attn_w_out[j].astype(BF16)
            proj = _attn_inproj_call(x, mod_lat, g_row, w_in, cos, sin_signed,
                                     kinds=attn_kinds, rope=True, row_tile=INPROJ_ROW_TILE)
            if ctx_out:
                proj_c = split(_attn_inproj_call(merged(ctx_stream), mod_ctx, g_row, w_in, ones_ctx, zeros_ctx,
                                                 kinds=attn_kinds, rope=False, row_tile=ROW_TILE))
                kv_col = K_COL
                o_c = _attn_call(attn_sink[j], proj_c, proj_c, window=False, ctx_kv_col=kv_col)
                ctx_next = split(_outproj_call(merged(o_c), w_out, merged(ctx_stream), mod_ctx, row_tile=ROW_TILE))
            else:
                proj_c = split(_attn_inproj_call(merged(ctx_stream), mod_ctx, g_row, w_in[:, K_COL:], ones_ctx,
                                                 zeros_ctx, kinds=("k", "v"), rope=False, row_tile=ROW_TILE))
                kv_col = 0
                ctx_next = None
            o = _attn_call(attn_sink[j], proj, proj_c, window=True, ctx_kv_col=kv_col)
            x = _outproj_call(o, w_out, x, mod_lat, row_tile=INPROJ_ROW_TILE)
        else:
            w_in = conv_w_in[j].astype(BF16)
            w_out = conv_w_out[j].astype(BF16)
            bias = conv_b_in[j].reshape(1, 3 * CONV_WIDTH)
            dw_w = _lane_chunked(jnp.repeat(conv_dw_w[j].astype(BF16), BF16_ROWS, axis=0))
            conv_args = (dw_w, _lane_chunked(conv_dw_b[j][None, :]),
                         _lane_chunked(conv_ln_g[j][None, :]), _lane_chunked(conv_ln_b[j][None, :]), w_out)
            final = i == DEPTH - 1
            if ctx_out:
                u_c, z_c = map(split, _conv_inproj_call(merged(ctx_stream), mod_ctx, g_row, w_in, bias,
                                                        row_tile=ROW_TILE))
                ctx_next = _conv_mix_call(u_c, z_c, *conv_args, ctx_stream, mod_ctx, final_row,
                                          row_tile=n_ctx, final=False)
            else:
                ctx_next = None
            u, z = _conv_inproj_call(x, mod_lat, g_row, w_in, bias, row_tile=INPROJ_ROW_TILE)
            x = _conv_mix_call(u, z, *conv_args, x, mod_lat, final_row, row_tile=ROW_TILE, final=final)
        if ctx_next is not None:
            ctx_stream = ctx_next
    return x
```

```python
import functools
import math

import jax
import jax.numpy as jnp
from jax import lax
from jax.experimental import pallas as pl
from jax.experimental.pallas import tpu as pltpu

D_MODEL = 1024
DEPTH = 4
GRID_W = 64
N_Q_HEADS = 16
N_KV_HEADS = 4
GROUP = N_Q_HEADS // N_KV_HEADS
HEAD_DIM = 128
ATTN_WIDTH = N_Q_HEADS * HEAD_DIM
KV_WIDTH = N_KV_HEADS * HEAD_DIM
ATTN_PROJ = 2 * ATTN_WIDTH + 2 * KV_WIDTH
WINDOW = 128
ROPE_BASE = 10000.0
CONV_WIDTH = 2 * D_MODEL
CONV_K = 31
CONV_HALF = CONV_K // 2
EPS = 1e-6
LOG2E = math.log2(math.e)

V7X_LANES = 128
V7X_SUBLANES = 8
V7X_VMEM_LIMIT_BYTES = 56 * 1024 * 1024

ROW_TILE = 512
INPROJ_ROW_TILE = 1024
COL_CHUNK = 512
Q_TILE = 2 * WINDOW
ATTN_TILES_PER_STEP = 2
CONV_HALO = 16
CONV_ROW_CHUNK = 64
N_LANE_CHUNKS = CONV_WIDTH // V7X_LANES
BF16_ROWS = 16

Q_COL, Z_COL, K_COL, V_COL = 0, ATTN_WIDTH, 2 * ATTN_WIDTH, 2 * ATTN_WIDTH + KV_WIDTH

BF16 = jnp.bfloat16
F32 = jnp.float32
U32 = jnp.uint32


def _params(*sem):
    return pltpu.CompilerParams(dimension_semantics=sem, vmem_limit_bytes=V7X_VMEM_LIMIT_BYTES)


def _silu(t):
    h = 0.5 * t
    return h + h * jnp.tanh(h)


def _resident(shape):
    nd = len(shape)
    return pl.BlockSpec(shape, lambda *_: (0,) * nd, pipeline_mode=pl.Buffered(1))


def _ada_kernel(c_ref, w_ref, b_ref, o_ref):
    a = _silu(c_ref[...])
    o_ref[0] = jnp.dot(a.astype(BF16), w_ref[0].astype(BF16), preferred_element_type=F32) + b_ref[0]


def _ada_call(c_rows, ada_w, ada_b):
    n_rows = c_rows.shape[0]
    tn = D_MODEL
    return pl.pallas_call(
        _ada_kernel,
        grid=(DEPTH, 3 * D_MODEL // tn),
        in_specs=[
            pl.BlockSpec((n_rows, D_MODEL), lambda i, j: (0, 0)),
            pl.BlockSpec((1, D_MODEL, tn), lambda i, j: (i, 0, j)),
            pl.BlockSpec((1, 1, tn), lambda i, j: (i, 0, j)),
        ],
        out_specs=pl.BlockSpec((1, n_rows, tn), lambda i, j: (i, 0, j)),
        out_shape=jax.ShapeDtypeStruct((DEPTH, n_rows, 3 * D_MODEL), F32),
        compiler_params=_params("arbitrary", "arbitrary"),
        name="ada_mod",
    )(c_rows, ada_w, ada_b.reshape(DEPTH, 1, 3 * D_MODEL))


def _modulated_norm(x, g_row, mod):
    ms = jnp.mean(x * x, axis=-1, keepdims=True)
    return x * lax.rsqrt(ms + EPS) * (g_row * (1.0 + mod[1:2, :])) + mod[0:1, :]


def _rope(t, cos, sin_signed):
    heads = []
    for h in range(t.shape[1] // HEAD_DIM):
        th = t[:, h * HEAD_DIM:(h + 1) * HEAD_DIM]
        heads.append(th * cos + pltpu.roll(th, HEAD_DIM // 2, 1) * sin_signed)
    return jnp.concatenate(heads, axis=1)


def _attn_inproj_kernel(x_ref, mod_ref, g_ref, w_ref, cos_ref, sin_ref, o_ref, *, kinds, rope):
    hb = _modulated_norm(x_ref[0], g_ref[...], mod_ref[0]).astype(BF16)
    for j, kind in enumerate(kinds):
        cols = slice(j * COL_CHUNK, (j + 1) * COL_CHUNK)
        t = jnp.dot(hb, w_ref[:, cols], preferred_element_type=F32)
        if rope and kind in ("q", "k"):
            t = _rope(t, cos_ref[...], sin_ref[...])
        if kind == "q":
            t = t * (HEAD_DIM ** -0.5 * LOG2E)
        if kind == "z":
            t = _silu(t)
        o_ref[0, :, cols] = t.astype(BF16)


def _attn_inproj_call(x, mod, g_row, w, cos, sin_signed, *, kinds, rope, row_tile):
    b, n, _ = x.shape
    width = len(kinds) * COL_CHUNK
    assert w.shape == (D_MODEL, width) and n % row_tile == 0
    mod_map = (lambda bi, ti: (bi, 0, 0)) if mod.shape[0] == b else (lambda bi, ti: (0, 0, 0))
    return pl.pallas_call(
        functools.partial(_attn_inproj_kernel, kinds=kinds, rope=rope),
        grid=(b, n // row_tile),
        in_specs=[
            pl.BlockSpec((1, row_tile, D_MODEL), lambda bi, ti: (bi, ti, 0)),
            pl.BlockSpec((1, 3, D_MODEL), mod_map),
            _resident((1, D_MODEL)),
            _resident((D_MODEL, width)),
            pl.BlockSpec((row_tile, HEAD_DIM), lambda bi, ti: (ti, 0)),
            pl.BlockSpec((row_tile, HEAD_DIM), lambda bi, ti: (ti, 0)),
        ],
        out_specs=pl.BlockSpec((1, row_tile, width), lambda bi, ti: (bi, ti, 0)),
        out_shape=jax.ShapeDtypeStruct((b, n, width), BF16),
        compiler_params=_params("parallel", "parallel"),
        name="attn_inproj",
    )(x, mod, g_row, w, cos, sin_signed)


def _dot_nt(a, b):
    return lax.dot_general(a, b, (((1,), (1,)), ((), ())), preferred_element_type=F32)


def _attn_kernel(sink_ref, q_ref, z_ref, kvc_ref, *rest, window, q_tile, n_ctx):
    if window:
        kvp_ref, kvm_ref, kvn_ref, bias_ref, o_ref, k_all, v_all = rest
        ti = pl.program_id(1)
        n_steps = pl.num_programs(1)
        n_sub = ATTN_TILES_PER_STEP
    else:
        (o_ref,) = rest
        n_sub = 1

    for t in range(n_sub):
        q_rows = slice(t * q_tile, (t + 1) * q_tile)
        if window:
            lo, hi = t * q_tile, (t + 1) * q_tile
            before = (kvp_ref, slice(0, WINDOW)) if t == 0 else (kvm_ref, slice(lo - WINDOW, lo))
            after = (kvn_ref, slice(0, WINDOW)) if t == n_sub - 1 else (kvm_ref, slice(hi, hi + WINDOW))
            kv_parts = ((kvc_ref, slice(0, n_ctx)), before, (kvm_ref, slice(lo, hi)), after)
            row = 0
            for ref, rs in kv_parts:
                k_all[row:row + rs.stop - rs.start, :] = ref[0, rs, :KV_WIDTH]
                row += rs.stop - rs.start
            lane = lax.broadcasted_iota(jnp.int32, (k_all.shape[0], HEAD_DIM), 1)
            ones_col = jnp.where(lane == 0, 1.0, 0.0).astype(BF16)
            for hk in range(N_KV_HEADS):
                row = 0
                for ref, rs in kv_parts:
                    v_all[row:row + rs.stop - rs.start, 2 * hk * HEAD_DIM:(2 * hk + 1) * HEAD_DIM] = (
                        ref[0, rs, KV_WIDTH + hk * HEAD_DIM:KV_WIDTH + (hk + 1) * HEAD_DIM])
                    row += rs.stop - rs.start
                v_all[:, (2 * hk + 1) * HEAD_DIM:(2 * hk + 2) * HEAD_DIM] = ones_col
            bias_idx = jnp.int32(1)
            if t == 0:
                bias_idx = jnp.where(ti == 0, 0, bias_idx)
            if t == n_sub - 1:
                bias_idx = jnp.where(ti == n_steps - 1, 2, bias_idx)

        for hk in range(N_KV_HEADS):
            kv_cols = slice(hk * HEAD_DIM, (hk + 1) * HEAD_DIM)
            qs = jnp.concatenate(
                [q_ref[0, q_rows, (hk * GROUP + g) * HEAD_DIM:(hk * GROUP + g + 1) * HEAD_DIM]
                 for g in range(GROUP)], axis=0)
            k = k_all[:, kv_cols] if window else kvc_ref[0, :, kv_cols]
            v = (v_all[:, 2 * hk * HEAD_DIM:(2 * hk + 2) * HEAD_DIM] if window
                 else kvc_ref[0, :, KV_WIDTH + hk * HEAD_DIM:KV_WIDTH + (hk + 1) * HEAD_DIM])
            s = _dot_nt(qs, k)
            probs, denoms = [], []
            for g in range(GROUP):
                sg = s[g * q_tile:(g + 1) * q_tile, :]
                sink = jnp.full((q_tile, 1), sink_ref[hk * GROUP + g] * LOG2E, F32)
                if window:
                    for h0 in range(0, q_tile, WINDOW):
                        rows = slice(h0, h0 + WINDOW)
                        pieces = [sg[rows, :n_ctx]]
                        for c in range(0, q_tile + 2 * WINDOW, WINDOW):
                            lo, hi = c - WINDOW - (h0 + WINDOW - 1), c + WINDOW - 1 - WINDOW - h0
                            if hi < -WINDOW or lo > WINDOW:
                                pieces.append(None)
                            elif lo >= -WINDOW and hi <= WINDOW and c != 0 and c != q_tile + WINDOW:
                                pieces.append(sg[rows, n_ctx + c:n_ctx + c + WINDOW])
                            else:
                                pieces.append(sg[rows, n_ctx + c:n_ctx + c + WINDOW]
                                              + bias_ref[bias_idx, rows, c:c + WINDOW])
                        m = sink[rows]
                        for pc in pieces:
                            if pc is not None:
                                m = jnp.maximum(m, jnp.max(pc, axis=-1, keepdims=True))
                        ps = [jnp.zeros((WINDOW, WINDOW), BF16) if pc is None else jnp.exp2(pc - m).astype(BF16)
                              for pc in pieces]
                        probs.append(jnp.concatenate(ps, axis=1))
                        denoms.append(jnp.exp2(sink[rows] - m))
                else:
                    m = jnp.maximum(jnp.max(sg, axis=-1, keepdims=True), sink)
                    p = jnp.exp2(sg - m)
                    denoms.append(jnp.exp2(sink - m) + jnp.sum(p, axis=-1, keepdims=True))
                    probs.append(p.astype(BF16))
            o = jnp.dot(jnp.concatenate(probs, axis=0), v, preferred_element_type=F32)
            if window:
                o = o[:, :HEAD_DIM] / (jnp.concatenate(denoms, axis=0) + o[:, HEAD_DIM:HEAD_DIM + 1])
            for g in range(GROUP):
                cols = slice((hk * GROUP + g) * HEAD_DIM, (hk * GROUP + g + 1) * HEAD_DIM)
                gate = z_ref[0, q_rows, cols].astype(F32)
                og = o[g * q_tile:(g + 1) * q_tile, :]
                if not window:
                    og = og / denoms[g]
                o_ref[0, q_rows, cols] = (og * gate).astype(BF16)


def _band_bias(q_tile):
    n_win = q_tile + 2 * WINDOW
    r = jnp.arange(q_tile)[:, None]
    j = jnp.arange(n_win)[None, :]
    band = jnp.abs(j - WINDOW - r) <= WINDOW
    first = band & (j >= WINDOW)
    last = band & (j < WINDOW + q_tile)
    neg = jnp.float32(-jnp.inf)
    return jnp.stack([jnp.where(msk, 0.0, neg).astype(F32) for msk in (first, band, last)])


def _attn_call(sink, proj, proj_ctx, *, window, ctx_kv_col):
    b, n, _ = proj.shape
    n_ctx = proj_ctx.shape[1]
    q_tile = Q_TILE if window else n
    step_rows = ATTN_TILES_PER_STEP * q_tile if window else n
    n_steps = n // step_rows
    assert n % step_rows == 0 and (not window or n // q_tile >= 2)
    half_per_step = step_rows // WINDOW
    n_half = n // WINDOW
    kv_width = 2 * KV_WIDTH
    kv_col = K_COL // kv_width
    in_specs = [
        pl.BlockSpec(memory_space=pltpu.SMEM),
        pl.BlockSpec((1, step_rows, ATTN_WIDTH), lambda bi, ti: (bi, ti, Q_COL // ATTN_WIDTH)),
        pl.BlockSpec((1, step_rows, ATTN_WIDTH), lambda bi, ti: (bi, ti, Z_COL // ATTN_WIDTH)),
        pl.BlockSpec((1, n_ctx, kv_width), lambda bi, ti: (bi, 0, ctx_kv_col // kv_width)),
    ]
    args = [sink, proj, proj, proj_ctx]
    scratch = []
    if window:
        in_specs += [
            pl.BlockSpec((1, WINDOW, kv_width), lambda bi, ti: (bi, jnp.maximum(ti * half_per_step - 1, 0), kv_col)),
            pl.BlockSpec((1, step_rows, kv_width), lambda bi, ti: (bi, ti, kv_col)),
            pl.BlockSpec((1, WINDOW, kv_width),
                         lambda bi, ti: (bi, jnp.minimum((ti + 1) * half_per_step, n_half - 1), kv_col)),
            _resident((3, q_tile, q_tile + 2 * WINDOW)),
        ]
        args += [proj, proj, proj, _band_bias(q_tile)]
        n_keys = n_ctx + q_tile + 2 * WINDOW
        scratch = [pltpu.VMEM((n_keys, KV_WIDTH), BF16), pltpu.VMEM((n_keys, 2 * KV_WIDTH), BF16)]
    return pl.pallas_call(
        functools.partial(_attn_kernel, window=window, q_tile=q_tile, n_ctx=n_ctx),
        grid=(b, n_steps),
        in_specs=in_specs,
        out_specs=pl.BlockSpec((1, step_rows, ATTN_WIDTH), lambda bi, ti: (bi, ti, 0)),
        out_shape=jax.ShapeDtypeStruct((b, n, ATTN_WIDTH), BF16),
        scratch_shapes=scratch,
        compiler_params=_params("parallel", "arbitrary"),
        name="attn_window" if window else "attn_ctx",
    )(*args)


def _residual_out(x, y, mod, final_g):
    out = x + mod[2:3, :] * y
    if final_g is not None:
        ms = jnp.mean(out * out, axis=-1, keepdims=True)
        out = out * lax.rsqrt(ms + EPS) * final_g
    return out


def _outproj_kernel(a_ref, w_ref, x_ref, mod_ref, o_ref):
    y = jnp.dot(a_ref[0], w_ref[...], preferred_element_type=F32)
    o_ref[0] = _residual_out(x_ref[0], y, mod_ref[0], None)


def _outproj_call(a, w, x, mod, *, row_tile):
    b, n, width = a.shape
    mod_map = (lambda bi, ti: (bi, 0, 0)) if mod.shape[0] == b else (lambda bi, ti: (0, 0, 0))
    return pl.pallas_call(
        _outproj_kernel,
        grid=(b, n // row_tile),
        in_specs=[
            pl.BlockSpec((1, row_tile, width), lambda bi, ti: (bi, ti, 0)),
            _resident((width, D_MODEL)),
            pl.BlockSpec((1, row_tile, D_MODEL), lambda bi, ti: (bi, ti, 0)),
            pl.BlockSpec((1, 3, D_MODEL), mod_map),
        ],
        out_specs=pl.BlockSpec((1, row_tile, D_MODEL), lambda bi, ti: (bi, ti, 0)),
        out_shape=jax.ShapeDtypeStruct((b, n, D_MODEL), F32),
        compiler_params=_params("parallel", "parallel"),
        name="attn_outproj",
    )(a, w, x, mod)


def _conv_inproj_kernel(x_ref, mod_ref, g_ref, w_ref, b_ref, u_ref, z_ref):
    hb = _modulated_norm(x_ref[0], g_ref[...], mod_ref[0]).astype(BF16)
    for j in range(CONV_WIDTH // COL_CHUNK):
        ca = slice(j * COL_CHUNK, (j + 1) * COL_CHUNK)
        cg = slice(CONV_WIDTH + j * COL_CHUNK, CONV_WIDTH + (j + 1) * COL_CHUNK)
        cz = slice(2 * CONV_WIDTH + j * COL_CHUNK, 2 * CONV_WIDTH + (j + 1) * COL_CHUNK)
        a = jnp.dot(hb, w_ref[:, ca], preferred_element_type=F32) + b_ref[:, ca]
        gl = jnp.dot(hb, w_ref[:, cg], preferred_element_type=F32) + b_ref[:, cg]
        u_ref[0, :, ca] = (a * jax.nn.sigmoid(gl)).astype(BF16)
        z = jnp.dot(hb, w_ref[:, cz], preferred_element_type=F32) + b_ref[:, cz]
        z_ref[0, :, ca] = _silu(z).astype(BF16)


def _conv_inproj_call(x, mod, g_row, w, bias, *, row_tile):
    b, n, _ = x.shape
    mod_map = (lambda bi, ti: (bi, 0, 0)) if mod.shape[0] == b else (lambda bi, ti: (0, 0, 0))
    out_spec = pl.BlockSpec((1, row_tile, CONV_WIDTH), lambda bi, ti: (bi, ti, 0))
    out_sds = jax.ShapeDtypeStruct((b, n, CONV_WIDTH), BF16)
    return pl.pallas_call(
        _conv_inproj_kernel,
        grid=(b, n // row_tile),
        in_specs=[
            pl.BlockSpec((1, row_tile, D_MODEL), lambda bi, ti: (bi, ti, 0)),
            pl.BlockSpec((1, 3, D_MODEL), mod_map),
            _resident((1, D_MODEL)),
            _resident((D_MODEL, 3 * CONV_WIDTH)),
            _resident((1, 3 * CONV_WIDTH)),
        ],
        out_specs=[out_spec, out_spec],
        out_shape=[out_sds, out_sds],
        compiler_params=_params("parallel", "parallel"),
        name="conv_inproj",
    )(x, mod, g_row, w, bias)


def _conv_mix_kernel(up_ref, um_ref, un_ref, z_ref, dww_ref, dwb_ref, lng_ref, lnb_ref, w_ref,
                     x_ref, mod_ref, fg_ref, o_ref, even_ref, odd_ref, act_ref, *, row_tile, final):
    ti = pl.program_id(1)
    n_tiles = pl.num_programs(1)
    half = CONV_HALO // 2
    prev = jnp.where(ti > 0, up_ref[0], jnp.zeros_like(up_ref[0]))
    nxt = jnp.where(ti < n_tiles - 1, un_ref[0], jnp.zeros_like(un_ref[0]))
    for c in range(N_LANE_CHUNKS):
        lanes = slice(c * V7X_LANES, (c + 1) * V7X_LANES)
        parts = (prev[:, lanes], um_ref[0, :, lanes], nxt[:, lanes])
        row = 0
        for part in parts:
            even_ref[c, row // 2:(row + part.shape[0]) // 2, :] = pltpu.bitcast(part, U32)
            row += part.shape[0]
        n_words = row // 2
        even_ref[c, n_words:, :] = jnp.zeros((even_ref.shape[1] - n_words, V7X_LANES), U32)
        odd_ref[c] = (even_ref[c, 0:n_words, :] >> 16) | (even_ref[c, 1:n_words + 1, :] << 16)

    first = CONV_HALO - CONV_HALF
    packed_per_chunk = CONV_ROW_CHUNK // BF16_ROWS

    def lane_chunk(ci, carry):
        for r0 in range(0, row_tile, CONV_ROW_CHUNK):
            acc = jnp.zeros((CONV_ROW_CHUNK, V7X_LANES), F32)
            for k in range(CONV_K):
                s = r0 + first + k
                src = odd_ref if s % 2 else even_ref
                word = s // 2
                taps = jnp.concatenate(
                    [pltpu.bitcast(src[ci, word + half * m:word + half * (m + 1), :], BF16)
                     for m in range(packed_per_chunk)], axis=0)
                wk = jnp.concatenate([dww_ref[ci, BF16_ROWS * k:BF16_ROWS * (k + 1), :]] * packed_per_chunk, axis=0)
                acc = acc + taps.astype(F32) * wk.astype(F32)
            act_ref[ci, r0:r0 + CONV_ROW_CHUNK, :] = acc + dwb_ref[ci]
        return carry

    lax.fori_loop(0, N_LANE_CHUNKS, lane_chunk, 0)

    inv_n = 1.0 / CONV_WIDTH
    u = act_ref[...]
    mu = jnp.sum(jnp.sum(u, axis=0), axis=-1, keepdims=True) * inv_n
    uc = u - mu[None]
    var = jnp.sum(jnp.sum(uc * uc, axis=0), axis=-1, keepdims=True) * inv_n
    rstd = lax.rsqrt(var + EPS)
    parts = []
    for c in range(N_LANE_CHUNKS):
        lanes = slice(c * V7X_LANES, (c + 1) * V7X_LANES)
        t = uc[c] * rstd * lng_ref[c] + lnb_ref[c]
        parts.append((_silu(t) * z_ref[0, :, lanes].astype(F32)).astype(BF16))
    a = jnp.concatenate(parts, axis=1)
    y = jnp.dot(a, w_ref[...], preferred_element_type=F32)
    o_ref[0] = _residual_out(x_ref[0], y, mod_ref[0], fg_ref[...] if final else None)


def _conv_mix_call(u, z, dw_w, dw_b, ln_g, ln_b, w_out, x, mod, final_g, *, row_tile, final):
    b, n, _ = u.shape
    assert n % row_tile == 0 and row_tile % CONV_HALO == 0 and row_tile % CONV_ROW_CHUNK == 0
    halo_per_tile = row_tile // CONV_HALO
    n_halo_blocks = n // CONV_HALO
    mod_map = (lambda bi, ti: (bi, 0, 0)) if mod.shape[0] == b else (lambda bi, ti: (0, 0, 0))
    return pl.pallas_call(
        functools.partial(_conv_mix_kernel, row_tile=row_tile, final=final),
        grid=(b, n // row_tile),
        in_specs=[
            pl.BlockSpec((1, CONV_HALO, CONV_WIDTH),
                         lambda bi, ti: (bi, jnp.maximum(ti * halo_per_tile - 1, 0), 0)),
            pl.BlockSpec((1, row_tile, CONV_WIDTH), lambda bi, ti: (bi, ti, 0)),
            pl.BlockSpec((1, CONV_HALO, CONV_WIDTH),
                         lambda bi, ti: (bi, jnp.minimum((ti + 1) * halo_per_tile, n_halo_blocks - 1), 0)),
            pl.BlockSpec((1, row_tile, CONV_WIDTH), lambda bi, ti: (bi, ti, 0)),
            _resident((N_LANE_CHUNKS, CONV_K * BF16_ROWS, V7X_LANES)),
            _resident((N_LANE_CHUNKS, 1, V7X_LANES)),
            _resident((N_LANE_CHUNKS, 1, V7X_LANES)),
            _resident((N_LANE_CHUNKS, 1, V7X_LANES)),
            _resident((CONV_WIDTH, D_MODEL)),
            pl.BlockSpec((1, row_tile, D_MODEL), lambda bi, ti: (bi, ti, 0)),
            pl.BlockSpec((1, 3, D_MODEL), mod_map),
            _resident((1, D_MODEL)),
        ],
        out_specs=pl.BlockSpec((1, row_tile, D_MODEL), lambda bi, ti: (bi, ti, 0)),
        out_shape=jax.ShapeDtypeStruct((b, n, D_MODEL), F32),
        scratch_shapes=[
            pltpu.VMEM((N_LANE_CHUNKS, (row_tile + 2 * CONV_HALO) // 2 + V7X_SUBLANES, V7X_LANES), U32),
            pltpu.VMEM((N_LANE_CHUNKS, (row_tile + 2 * CONV_HALO) // 2, V7X_LANES), U32),
            pltpu.VMEM((N_LANE_CHUNKS, row_tile, V7X_LANES), F32),
        ],
        compiler_params=_params("parallel", "arbitrary"),
        name="conv_mix",
    )(u, u, u, z, dw_w, dw_b, ln_g, ln_b, w_out, x, mod, final_g)


def _rope_tables(s):
    rows = s // GRID_W
    row = jnp.repeat(jnp.arange(rows), GRID_W).astype(F32)
    col = jnp.tile(jnp.arange(GRID_W), rows).astype(F32)
    n_axis = HEAD_DIM // 4
    inv = ROPE_BASE ** (-jnp.arange(n_axis, dtype=F32) / n_axis)
    ang = jnp.concatenate([row[:, None] * inv, col[:, None] * inv], axis=-1)
    cos, sin = jnp.cos(ang), jnp.sin(ang)
    return jnp.concatenate([cos, cos], axis=-1), jnp.concatenate([-sin, sin], axis=-1)


def _lane_chunked(p):
    rows = p.shape[0]
    return p.reshape(rows, N_LANE_CHUNKS, V7X_LANES).transpose(1, 0, 2)


def _attn_weight(w_in):
    kv_end = ATTN_WIDTH + 2 * KV_WIDTH
    return jnp.concatenate([w_in[:, :ATTN_WIDTH], w_in[:, kv_end:], w_in[:, ATTN_WIDTH:kv_end]],
                           axis=1).astype(BF16)


def kernel(x, c, ctx, c_ctx, ada_w, ada_b, norm_g, attn_w_in, attn_sink, attn_w_out,
           conv_w_in, conv_b_in, conv_dw_w, conv_dw_b, conv_ln_g, conv_ln_b, conv_w_out, final_g):
    b, s, _ = x.shape
    n_ctx = ctx.shape[1]
    assert DEPTH % 2 == 0
    cos, sin_signed = _rope_tables(s)
    ones_ctx = jnp.ones((ROW_TILE, HEAD_DIM), F32)
    zeros_ctx = jnp.zeros((ROW_TILE, HEAD_DIM), F32)
    assert (b * n_ctx) % ROW_TILE == 0

    def merged(t):
        return t.reshape(b * n_ctx // ROW_TILE, ROW_TILE, t.shape[-1])

    def split(t):
        return t.reshape(b, n_ctx, t.shape[-1])

    n_rows = -(-(b + 1) // V7X_SUBLANES) * V7X_SUBLANES
    c_rows = jnp.concatenate([c, c_ctx[None, :], jnp.zeros((n_rows - b - 1, D_MODEL), F32)], axis=0)
    mods = _ada_call(c_rows, ada_w, ada_b)
    mods = mods.reshape(DEPTH, n_rows, 3, D_MODEL)

    final_row = final_g.reshape(1, D_MODEL)
    attn_kinds = ("q",) * 4 + ("z",) * 4 + ("k", "v")
    ctx_stream = ctx
    for i in range(DEPTH):
        kind, j = i % 2, i // 2
        ctx_out = any(l % 2 == 0 for l in range(i + 1, DEPTH))
        mod_lat = mods[i, :b]
        mod_ctx = mods[i, b:b + 1]
        g_row = norm_g[i].reshape(1, D_MODEL)
        if kind == 0:
            w_in = _attn_weight(attn_w_in[j])
            w_out = attn_w_out[j].astype(BF16)
            proj = _attn_inproj_call(x, mod_lat, g_row, w_in, cos, sin_signed,
                                     kinds=attn_kinds, rope=True, row_tile=INPROJ_ROW_TILE)
            if ctx_out:
                proj_c = split(_attn_inproj_call(merged(ctx_stream), mod_ctx, g_row, w_in, ones_ctx, zeros_ctx,
                                                 kinds=attn_kinds, rope=False, row_tile=ROW_TILE))
                kv_col = K_COL
                o_c = _attn_call(attn_sink[j], proj_c, proj_c, window=False, ctx_kv_col=kv_col)
                ctx_next = split(_outproj_call(merged(o_c), w_out, merged(ctx_stream), mod_ctx, row_tile=ROW_TILE))
            else:
                proj_c = split(_attn_inproj_call(merged(ctx_stream), mod_ctx, g_row, w_in[:, K_COL:], ones_ctx,
                                                 zeros_ctx, kinds=("k", "v"), rope=False, row_tile=ROW_TILE))
                kv_col = 0
                ctx_next = None
            o = _attn_call(attn_sink[j], proj, proj_c, window=True, ctx_kv_col=kv_col)
            x = _outproj_call(o, w_out, x, mod_lat, row_tile=INPROJ_ROW_TILE)
        else:
            w_in = conv_w_in[j].astype(BF16)
            w_out = conv_w_out[j].astype(BF16)
            bias = conv_b_in[j].reshape(1, 3 * CONV_WIDTH)
            dw_w = _lane_chunked(jnp.repeat(conv_dw_w[j].astype(BF16), BF16_ROWS, axis=0))
            conv_args = (dw_w, _lane_chunked(conv_dw_b[j][None, :]),
                         _lane_chunked(conv_ln_g[j][None, :]), _lane_chunked(conv_ln_b[j][None, :]), w_out)
            final = i == DEPTH - 1
            if ctx_out:
                u_c, z_c = map(split, _conv_inproj_call(merged(ctx_stream), mod_ctx, g_row, w_in, bias,
                                                        row_tile=ROW_TILE))
                ctx_next = _conv_mix_call(u_c, z_c, *conv_args, ctx_stream, mod_ctx, final_row,
                                          row_tile=n_ctx, final=False)
            else:
                ctx_next = None
            u, z = _conv_inproj_call(x, mod_lat, g_row, w_in, bias, row_tile=INPROJ_ROW_TILE)
            x = _conv_mix_call(u, z, *conv_args, x, mod_lat, final_row, row_tile=ROW_TILE, final=final)
        if ctx_next is not None:
            ctx_stream = ctx_next
    return x
```
